```python
import jax, jax.numpy as jnp
from jax import lax
import numpy as np

D_MODEL = 2048
BATCH = 16
SEQ = 2048
DEPTH = 2

GRID_W = 64
MLSTM_HEADS = 8
MLSTM_HD = 128
MLSTM_W = MLSTM_HEADS * MLSTM_HD
MLSTM_CHUNK = 128
ATTN_HEADS = 8
ATTN_KV_HEADS = 2
ATTN_HD = 128
ATTN_GROUP = ATTN_HEADS // ATTN_KV_HEADS
ATTN_Q_W = ATTN_HEADS * ATTN_HD
ATTN_KV_W = ATTN_KV_HEADS * ATTN_HD
Q_BLOCK = 128
ROPE_THETA = 10000.0
ROPE_PAIRS = ATTN_HD // 4
SPLIT_SIZES = (MLSTM_W, MLSTM_W, MLSTM_W, MLSTM_W, 4 * MLSTM_HEADS,
               ATTN_Q_W, ATTN_KV_W, ATTN_KV_W, D_MODEL, D_MODEL)
IN_W = sum(SPLIT_SIZES)
N_GROUPS = 4
EXPERTS_PER_GROUP = 8
N_EXPERTS = N_GROUPS * EXPERTS_PER_GROUP
TOP_K = 2
D_FF = 512
EXPERT_BLOCK = 128
PLE_DIM = 256
EPS = 1e-6

kernel_name = "hybrid_mlstm_axial_gqa_hmoe_encoder"


def rmsnorm(x, g):
    xf = x.astype(jnp.float32)
    y = xf * lax.rsqrt(jnp.mean(xf * xf, axis=-1, keepdims=True) + EPS)
    return (y * g.astype(jnp.float32)).astype(x.dtype)


def mlstm_chunkwise(q, k, v, logi, logf):
    B, H, S, dk = q.shape
    dv = v.shape[-1]
    nc = S // MLSTM_CHUNK
    L = MLSTM_CHUNK
    to_c = lambda t: jnp.moveaxis(t.reshape(B, H, nc, L, *t.shape[3:]), 2, 0)
    qc, kc, vc, lic = to_c(q), to_c(k), to_c(v), to_c(logi)
    bc = jnp.cumsum(to_c(logf), axis=-1)
    tril = jnp.tril(jnp.ones((L, L), dtype=bool))

    def step(carry, xs):
        C, n, m = carry
        qi, ki, vi, li, bi = xs
        dmat = bi[..., :, None] - bi[..., None, :] + li[..., None, :]
        dmat = jnp.where(tril, dmat, -jnp.inf)
        inter = bi + m[..., None]
        m_t = jnp.maximum(jnp.max(dmat, axis=-1), inter)
        s = jnp.einsum("bhtd,bhsd->bhts", qi, ki) * jnp.exp(dmat - m_t[..., None])
        g = jnp.exp(inter - m_t)
        num = jnp.einsum("bhts,bhse->bhte", s, vi) + g[..., None] * jnp.einsum("bhtd,bhde->bhte", qi, C)
        den = jnp.sum(s, axis=-1) + g * jnp.einsum("bhtd,bhd->bht", qi, n)
        h = num / jnp.maximum(jnp.abs(den), jnp.exp(-m_t))[..., None]
        b_last = bi[..., -1]
        a = b_last[..., None] - bi + li
        m_new = jnp.maximum(b_last + m, jnp.max(a, axis=-1))
        decay = jnp.exp(b_last + m - m_new)
        wk = jnp.exp(a - m_new[..., None])
        C_new = decay[..., None, None] * C + jnp.einsum("bhs,bhsd,bhse->bhde", wk, ki, vi)
        n_new = decay[..., None] * n + jnp.einsum("bhs,bhsd->bhd", wk, ki)
        return (C_new, n_new, m_new), h

    init = (jnp.zeros((B, H, dk, dv), jnp.float32), jnp.zeros((B, H, dk), jnp.float32),
            jnp.zeros((B, H), jnp.float32))
    _, hs = lax.scan(step, init, (qc, kc, vc, lic, bc))
    return jnp.moveaxis(hs, 0, 2).reshape(B, H, S, dv)


def bidir_mlstm(q, k, v, o, gates_pre, gate_b, norm_g):
    B, S, _ = q.shape
    heads = lambda t: t.reshape(B, S, MLSTM_HEADS, MLSTM_HD).transpose(0, 2, 1, 3).astype(jnp.float32)
    qh = heads(q) * (MLSTM_HD ** -0.5)
    kh, vh = heads(k), heads(v)
    g = (gates_pre.astype(jnp.float32) + gate_b.astype(jnp.float32))
    g = g.reshape(B, S, 4, MLSTM_HEADS).transpose(2, 0, 3, 1)
    i_fw, i_bw, f_fw, f_bw = g[0], g[1], g[2], g[3]
    h_fw = mlstm_chunkwise(qh, kh, vh, i_fw, jax.nn.log_sigmoid(f_fw))
    flip = lambda t: jnp.flip(t, axis=2)
    h_bw = flip(mlstm_chunkwise(flip(qh), flip(kh), flip(vh), flip(i_bw),
                                jax.nn.log_sigmoid(flip(f_bw))))
    hsum = (h_fw + h_bw).transpose(0, 2, 1, 3)
    hn = rmsnorm(hsum, norm_g.reshape(MLSTM_HEADS, MLSTM_HD)).reshape(B, S, MLSTM_W)
    return hn.astype(q.dtype) * jax.nn.sigmoid(o)


def axial_rope(t, ang_row, ang_col):
    def rot(u, ang):
        a, b = jnp.split(u, 2, axis=-1)
        c = jnp.cos(ang)[None, :, None, :]
        s = jnp.sin(ang)[None, :, None, :]
        return jnp.concatenate([a * c - b * s, b * c + a * s], axis=-1)
    half = ATTN_HD // 2
    return jnp.concatenate([rot(t[..., :half], ang_row), rot(t[..., half:], ang_col)], axis=-1)


def grid_attention(q, k, v, q_g, k_g, ang_row, ang_col):
    B, S, _ = q.shape
    nb = S // Q_BLOCK
    qh = q.reshape(B, S, ATTN_HEADS, ATTN_HD)
    kh = k.reshape(B, S, ATTN_KV_HEADS, ATTN_HD)
    vh = v.reshape(B, S, ATTN_KV_HEADS, ATTN_HD)
    qh = axial_rope(rmsnorm(qh, q_g).astype(jnp.float32), ang_row, ang_col).astype(q.dtype)
    kh = axial_rope(rmsnorm(kh, k_g).astype(jnp.float32), ang_row, ang_col).astype(k.dtype)
    qb = qh.reshape(B, nb, Q_BLOCK, ATTN_KV_HEADS, ATTN_GROUP, ATTN_HD).transpose(1, 0, 3, 4, 2, 5)
    kt = kh.transpose(0, 2, 1, 3)
    vt = vh.transpose(0, 2, 1, 3)
    scale = ATTN_HD ** -0.5

    def block(qi):
        s = jnp.einsum("bhgqd,bhkd->bhgqk", qi, kt).astype(jnp.float32) * scale
        pr = jax.nn.softmax(s, axis=-1).astype(vt.dtype)
        return jnp.einsum("bhgqk,bhkd->bhgqd", pr, vt)

    o = lax.map(block, qb)
    return o.transpose(1, 0, 4, 2, 3, 5).reshape(B, S, ATTN_Q_W)


def hier_moe(h, w_rg, b_rg, w_re, b_re, w_g, w_u, w_d):
    B, S, D = h.shape
    T = B * S
    hf = h.reshape(T, D)
    gprob = jax.nn.softmax((hf @ w_rg).astype(jnp.float32) + b_rg.astype(jnp.float32), axis=-1)
    gp, gi = lax.top_k(gprob, 1)
    elog = ((hf @ w_re).astype(jnp.float32) + b_re.astype(jnp.float32)).reshape(T, N_GROUPS, EXPERTS_PER_GROUP)
    elog_g = jnp.take_along_axis(elog, gi[:, :, None], axis=1)[:, 0]
    ev, ei = lax.top_k(elog_g, TOP_K)
    wts = jax.nn.softmax(ev, axis=-1) * gp
    eid = (gi * EXPERTS_PER_GROUP + ei).reshape(-1).astype(jnp.int32)
    wflat = wts.reshape(-1)
    n_assign = T * TOP_K
    slot_ids = jnp.arange(n_assign, dtype=jnp.int32)
    tok = slot_ids // TOP_K
    order = jnp.argsort(eid)
    se, st, sw = eid[order], tok[order], wflat[order]
    counts = jnp.bincount(eid, length=N_EXPERTS).astype(jnp.int32)
    padded = ((counts + EXPERT_BLOCK - 1) // EXPERT_BLOCK) * EXPERT_BLOCK
    pad_end = jnp.cumsum(padded)
    pad_start = pad_end - padded
    start = jnp.cumsum(counts) - counts
    dest = pad_start[se] + slot_ids - start[se]
    P = n_assign + N_EXPERTS * EXPERT_BLOCK
    nblk = P // EXPERT_BLOCK
    slot_tok = jnp.full((P,), T, dtype=jnp.int32).at[dest].set(st)
    slot_w = jnp.zeros((P,), jnp.float32).at[dest].set(sw)
    blk_e = jnp.minimum(jnp.searchsorted(pad_end, jnp.arange(nblk, dtype=jnp.int32) * EXPERT_BLOCK,
                                         side="right"), N_EXPERTS - 1)
    x_pad = jnp.concatenate([hf, jnp.zeros((1, D), hf.dtype)], axis=0)

    def run(args):
        e, tk, wt = args
        xb = x_pad[tk]
        y = (jax.nn.silu(xb @ w_g[e]) * (xb @ w_u[e])) @ w_d[e]
        return y * wt[:, None].astype(y.dtype)

    ys = lax.map(run, (blk_e, slot_tok.reshape(nblk, EXPERT_BLOCK), slot_w.reshape(nblk, EXPERT_BLOCK)))
    out = jnp.zeros((T + 1, D), h.dtype).at[slot_tok].add(ys.reshape(P, D))
    return out[:T].reshape(B, S, D)


def setup_inputs(seed: int = 0) -> dict:
    key = jax.random.key(seed)
    ks = jax.random.split(key, 24)
    nrm = lambda k, shape, fan_in: jax.random.normal(k, shape, jnp.float32) * (fan_in ** -0.5)
    gain = lambda k, shape: 1.0 + 0.02 * jax.random.normal(k, shape, jnp.float32)
    i_bias = 0.1 * jax.random.normal(ks[5], (DEPTH, 2 * MLSTM_HEADS), jnp.float32)
    f_base = jnp.tile(jnp.linspace(3.0, 6.0, MLSTM_HEADS, dtype=jnp.float32), 2)
    f_bias = f_base[None, :] + 0.1 * jax.random.normal(ks[6], (DEPTH, 2 * MLSTM_HEADS), jnp.float32)
    return {
        "x": jax.random.normal(ks[0], (BATCH, SEQ, D_MODEL), jnp.float32),
        "p": jax.random.normal(ks[1], (DEPTH, BATCH, SEQ, PLE_DIM), jnp.float32),
        "norm1_g": gain(ks[2], (DEPTH, D_MODEL)),
        "w_in": nrm(ks[3], (DEPTH, D_MODEL, IN_W), D_MODEL),
        "mlstm_gate_b": jnp.concatenate([i_bias, f_bias], axis=-1),
        "mlstm_norm_g": gain(ks[4], (DEPTH, MLSTM_W)),
        "q_norm_g": gain(ks[7], (DEPTH, ATTN_HD)),
        "k_norm_g": gain(ks[8], (DEPTH, ATTN_HD)),
        "w_branch_m": nrm(ks[9], (DEPTH, MLSTM_W, D_MODEL), MLSTM_W),
        "w_branch_a": nrm(ks[10], (DEPTH, ATTN_Q_W, D_MODEL), ATTN_Q_W),
        "w_out": nrm(ks[11], (DEPTH, D_MODEL, D_MODEL), D_MODEL),
        "norm2_g": gain(ks[12], (DEPTH, D_MODEL)),
        "w_router_group": nrm(ks[13], (DEPTH, D_MODEL, N_GROUPS), D_MODEL),
        "b_router_group": 0.01 * jax.random.normal(ks[14], (DEPTH, N_GROUPS), jnp.float32),
        "w_router_expert": nrm(ks[15], (DEPTH, D_MODEL, N_EXPERTS), D_MODEL),
        "b_router_expert": 0.01 * jax.random.normal(ks[16], (DEPTH, N_EXPERTS), jnp.float32),
        "w_exp_gate": nrm(ks[17], (DEPTH, N_EXPERTS, D_MODEL, D_FF), D_MODEL),
        "w_exp_up": nrm(ks[18], (DEPTH, N_EXPERTS, D_MODEL, D_FF), D_MODEL),
        "w_exp_down": nrm(ks[19], (DEPTH, N_EXPERTS, D_FF, D_MODEL), D_FF),
        "ple_norm_g": gain(ks[20], (DEPTH, D_MODEL)),
        "w_ple_gate": nrm(ks[21], (DEPTH, D_MODEL, D_MODEL), D_MODEL),
        "w_ple_proj": nrm(ks[22], (DEPTH, PLE_DIM, D_MODEL), PLE_DIM),
        "final_norm_g": gain(ks[23], (D_MODEL,)),
    }


def reference(x, p, norm1_g, w_in, mlstm_gate_b, mlstm_norm_g, q_norm_g, k_norm_g,
              w_branch_m, w_branch_a, w_out, norm2_g, w_router_group, b_router_group,
              w_router_expert, b_router_expert, w_exp_gate, w_exp_up, w_exp_down,
              ple_norm_g, w_ple_gate, w_ple_proj, final_norm_g):
    S = x.shape[1]
    ROWS = S // GRID_W
    row = jnp.repeat(jnp.arange(ROWS, dtype=jnp.float32), GRID_W)
    col = jnp.tile(jnp.arange(GRID_W, dtype=jnp.float32), ROWS)
    inv_freq = ROPE_THETA ** (-jnp.arange(ROPE_PAIRS, dtype=jnp.float32) / ROPE_PAIRS)
    ang_row = row[:, None] * inv_freq[None, :]
    ang_col = col[:, None] * inv_freq[None, :]
    split_idx = np.cumsum(SPLIT_SIZES)[:-1].tolist()

    for i in range(DEPTH):
        h = rmsnorm(x, norm1_g[i])
        z = h @ w_in[i]
        q_m, k_m, v_m, o_m, g_m, q_a, k_a, v_a, gate_m, gate_a = jnp.split(z, split_idx, axis=-1)
        y_m = bidir_mlstm(q_m, k_m, v_m, o_m, g_m, mlstm_gate_b[i], mlstm_norm_g[i])
        y_a = grid_attention(q_a, k_a, v_a, q_norm_g[i], k_norm_g[i], ang_row, ang_col)
        merged = (jax.nn.sigmoid(gate_m) * (y_m @ w_branch_m[i])
                  + jax.nn.sigmoid(gate_a) * (y_a @ w_branch_a[i]))
        x = x + merged @ w_out[i]
        x = x + hier_moe(rmsnorm(x, norm2_g[i]), w_router_group[i], b_router_group[i],
                         w_router_expert[i], b_router_expert[i],
                         w_exp_gate[i], w_exp_up[i], w_exp_down[i])
        ple_gate = jax.nn.sigmoid(rmsnorm(x, ple_norm_g[i]) @ w_ple_gate[i])
        x = x + ple_gate * (p[i] @ w_ple_proj[i])
    return rmsnorm(x, final_norm_g)
```

```python
import functools

import jax
import jax.numpy as jnp
import numpy as np
from jax import lax
from jax.experimental import pallas as pl
from jax.experimental.pallas import tpu as pltpu

F32 = jnp.float32
BF16 = jnp.bfloat16

GRID_W = 64
HEAD_DIM = 128
MLSTM_HEADS = 8
MLSTM_W = MLSTM_HEADS * HEAD_DIM
MLSTM_CHUNK = 128
ATTN_HEADS = 8
ATTN_KV_HEADS = 2
ATTN_GROUP = ATTN_HEADS // ATTN_KV_HEADS
ATTN_Q_W = ATTN_HEADS * HEAD_DIM
ATTN_KV_W = ATTN_KV_HEADS * HEAD_DIM
ROPE_THETA = 10000.0
ROPE_PAIRS = HEAD_DIM // 4
N_GROUPS = 4
EXPERTS_PER_GROUP = 8
N_EXPERTS = N_GROUPS * EXPERTS_PER_GROUP
EXPERT_BLOCK = 128
EPS = 1e-6
GATE_LANES = 128
ROUTER_ROWS = 128
ROUTER_EXPERT_ROW0 = 8
VMEM_LIMIT_BYTES = 52 * 1024 * 1024
NORM_ROWS = 128


def _params(semantics, **kw):
    return pltpu.CompilerParams(dimension_semantics=semantics,
                                vmem_limit_bytes=VMEM_LIMIT_BYTES, **kw)


def _rms(x, g):
    ms = jnp.mean(x * x, axis=-1, keepdims=True)
    return (x * lax.rsqrt(ms + EPS)) * g


def _norm_rows_into(x_ref, g_ref, h_scr):
    rows = x_ref.shape[0]
    step = min(NORM_ROWS, rows)

    def body(c, carry):
        r = pl.ds(pl.multiple_of(c * step, step), step)
        h_scr[r, :] = _rms(x_ref[r, :], g_ref[...]).astype(h_scr.dtype)
        return carry

    lax.fori_loop(0, rows // step, body, 0)


def _inproj_kernel(x_ref, g_ref, w_ref, wg_ref, z_ref, gates_ref, h_scr):
    @pl.when(pl.program_id(1) == 0)
    def _():
        _norm_rows_into(x_ref, g_ref, h_scr)
        gates_ref[...] = jnp.dot(h_scr[...], wg_ref[...], preferred_element_type=F32)

    z_ref[...] = jnp.dot(h_scr[...], w_ref[...], preferred_element_type=F32).astype(z_ref.dtype)


def _inproj(x, g, w, wg, tm, tn):
    T, D = x.shape
    N = w.shape[1]
    tm, tn = min(tm, T), min(tn, N)
    return pl.pallas_call(
        _inproj_kernel,
        grid=(T // tm, N // tn),
        in_specs=[pl.BlockSpec((tm, D), lambda i, j: (i, 0)),
                  pl.BlockSpec((1, D), lambda i, j: (0, 0)),
                  pl.BlockSpec((D, tn), lambda i, j: (0, j)),
                  pl.BlockSpec((D, GATE_LANES), lambda i, j: (0, 0))],
        out_specs=[pl.BlockSpec((tm, tn), lambda i, j: (i, j)),
                   pl.BlockSpec((tm, GATE_LANES), lambda i, j: (i, 0))],
        out_shape=[jax.ShapeDtypeStruct((T, N), BF16),
                   jax.ShapeDtypeStruct((T, GATE_LANES), F32)],
        scratch_shapes=[pltpu.VMEM((tm, D), BF16)],
        compiler_params=_params(("parallel", "arbitrary")),
        name="inproj",
    )(x, g, w, wg)


def _log_sigmoid(x):
    return jnp.minimum(x, 0.0) - jnp.log1p(jnp.exp(-jnp.abs(x)))


def _mlstm_kernel(q_ref, k_ref, v_ref, o_ref, gr_ref, gb_ref, ng_ref, y_ref,
                  hf_scr, hb_scr, c_scr):
    L = MLSTM_CHUNK
    S = q_ref.shape[1]
    nc = S // L
    lane = lax.broadcasted_iota(jnp.int32, (8, L), 1)
    sub = lax.broadcasted_iota(jnp.int32, (8, L), 0)
    tt = lax.broadcasted_iota(jnp.int32, (L, L), 0)
    ss = lax.broadcasted_iota(jnp.int32, (L, L), 1)
    mask_fw = ss <= tt
    mask_bw = ss >= tt
    c_scr[...] = jnp.zeros_like(c_scr)

    def gate_rows(c):
        g = gr_ref[0, 0, c] + gb_ref[0]
        ls = _log_sigmoid(g)
        cf = ls
        cb = ls
        k = 1
        while k < L:
            cf = cf + jnp.where(lane >= k, pltpu.roll(cf, k, 1), 0.0)
            cb = cb + jnp.where(lane < L - k, pltpu.roll(cb, L - k, 1), 0.0)
            k *= 2
        bfw = jnp.broadcast_to(cf[2:3], (8, L))
        bbw = jnp.broadcast_to(cb[3:4], (8, L))
        ifw = jnp.broadcast_to(g[0:1], (8, L))
        ibw = jnp.broadcast_to(g[1:2], (8, L))
        return jnp.where(sub == 0, bfw,
               jnp.where(sub == 1, ifw - bfw,
               jnp.where(sub == 2, bbw,
               jnp.where(sub == 3, ibw - bbw, 0.0))))

    def direction(c, rows, cols, r0, mask, last, d, n, m, h_scr):
        sl = pl.ds(pl.multiple_of(c * L, L), L)
        qi = q_ref[0, sl, :]
        ki = k_ref[0, sl, :]
        vi = v_ref[0, sl, :]
        bcol = cols[:, r0:r0 + 1]
        ucol = cols[:, r0 + 1:r0 + 2]
        urow = rows[r0 + 1:r0 + 2, :]
        b_last = bcol[last:last + 1, :]
        dm = jnp.where(mask, bcol + urow, -jnp.inf)
        inter = bcol + m
        m_t = jnp.maximum(jnp.max(dm, axis=1, keepdims=True), inter)
        s = lax.dot_general(qi, ki, (((1,), (1,)), ((), ())),
                            preferred_element_type=F32) * jnp.exp(dm - m_t)
        gq = jnp.exp(inter - m_t)
        C = c_scr[d]
        num = (jnp.dot(s.astype(BF16), vi, preferred_element_type=F32)
               + gq * jnp.dot(qi, C.astype(BF16), preferred_element_type=F32))
        qn = jnp.sum(qi.astype(F32) * n, axis=1, keepdims=True)
        den = jnp.sum(s, axis=1, keepdims=True) + gq * qn
        h_scr[sl, :] = num / jnp.maximum(jnp.abs(den), jnp.exp(-m_t))
        a = b_last + ucol
        m_new = jnp.maximum(b_last + m, jnp.max(a, axis=0, keepdims=True))
        decay = jnp.exp(b_last + m - m_new)
        kw = jnp.exp(a - m_new) * ki.astype(F32)
        c_scr[d] = decay * C + jnp.dot(kw.T.astype(BF16), vi, preferred_element_type=F32)
        n_new = decay * n + jnp.sum(kw, axis=0, keepdims=True)
        return n_new, m_new

    def body(j, carry):
        n_f, m_f, n_b, m_b = carry
        cf = j
        cb = nc - 1 - j
        rows_f = gate_rows(cf)
        rows_b = gate_rows(cb)
        pad = jnp.zeros((L - 8, L), F32)
        cols_f = jnp.concatenate([rows_f, pad], axis=0).T
        cols_b = jnp.concatenate([rows_b, pad], axis=0).T
        n_f, m_f = direction(cf, rows_f, cols_f, 0, mask_fw, L - 1, 0, n_f, m_f, hf_scr)
        n_b, m_b = direction(cb, rows_b, cols_b, 2, mask_bw, 0, 1, n_b, m_b, hb_scr)
        return n_f, m_f, n_b, m_b

    zn = jnp.zeros((1, HEAD_DIM), F32)
    zm = jnp.zeros((1, 1), F32)
    lax.fori_loop(0, nc, body, (zn, zm, zn, zm))

    def finish(c, carry):
        sl = pl.ds(pl.multiple_of(c * L, L), L)
        hn = _rms(hf_scr[sl, :] + hb_scr[sl, :], ng_ref[...])
        o = o_ref[0, sl, :].astype(F32)
        y_ref[0, sl, :] = (hn * jax.nn.sigmoid(o)).astype(y_ref.dtype)
        return carry

    lax.fori_loop(0, nc, finish, 0)


def _mlstm(z3, gate_rows, gate_bias_rows, norm_g):
    B, S, _ = z3.shape
    H = MLSTM_HEADS
    nc = S // MLSTM_CHUNK
    col = lambda off: pl.BlockSpec((1, S, HEAD_DIM), lambda b, h: (b, 0, off + h))
    return pl.pallas_call(
        _mlstm_kernel,
        grid=(B, H),
        in_specs=[col(0), col(H), col(2 * H), col(3 * H),
                  pl.BlockSpec((1, 1, nc, 8, MLSTM_CHUNK), lambda b, h: (b, h, 0, 0, 0)),
                  pl.BlockSpec((1, 8, MLSTM_CHUNK), lambda b, h: (h, 0, 0)),
                  pl.BlockSpec((1, HEAD_DIM), lambda b, h: (0, h))],
        out_specs=pl.BlockSpec((1, S, HEAD_DIM), lambda b, h: (b, 0, h)),
        out_shape=jax.ShapeDtypeStruct((B, S, MLSTM_W), BF16),
        scratch_shapes=[pltpu.VMEM((S, HEAD_DIM), F32), pltpu.VMEM((S, HEAD_DIM), F32),
                        pltpu.VMEM((2, HEAD_DIM, HEAD_DIM), F32)],
        compiler_params=_params(("parallel", "arbitrary")),
        name="mlstm",
    )(z3, z3, z3, z3, gate_rows, gate_bias_rows, norm_g)


def _rope(t, c, s_lo, s_hi):
    quarter = HEAD_DIM // 4
    return (t * c + pltpu.roll(t, HEAD_DIM - quarter, 1) * s_lo
            + pltpu.roll(t, quarter, 1) * s_hi)


def _attn_kernel(q_ref, k_ref, v_ref, cq_ref, sloq_ref, shiq_ref, ck_ref, slok_ref, shik_ref,
                 qg_ref, kg_ref, o_ref, k_scr):
    S = k_ref.shape[1]
    step = min(256, S)

    @pl.when((pl.program_id(2) == 0) & (pl.program_id(3) == 0))
    def _():
        def body(c, carry):
            r = pl.ds(pl.multiple_of(c * step, step), step)
            kn = _rms(k_ref[0, r, :].astype(F32), kg_ref[...])
            k_scr[r, :] = _rope(kn, ck_ref[r, :], slok_ref[r, :], shik_ref[r, :]).astype(BF16)
            return carry

        lax.fori_loop(0, S // step, body, 0)

    qn = _rms(q_ref[0].astype(F32), qg_ref[...])
    qr = _rope(qn, cq_ref[...], sloq_ref[...], shiq_ref[...]).astype(BF16)
    s = lax.dot_general(qr, k_scr[...], (((1,), (1,)), ((), ())), preferred_element_type=F32)
    p = jnp.exp(s - jnp.max(s, axis=1, keepdims=True))
    l = jnp.sum(p, axis=1, keepdims=True)
    o = jnp.dot(p.astype(BF16), v_ref[0], preferred_element_type=F32)
    o_ref[0] = (o / l).astype(o_ref.dtype)


def _attention(z3, q_blk0, tables, q_g, k_g, tq):
    B, S, _ = z3.shape
    tq = min(tq, S)
    cos_t, slo_t, shi_t = tables
    k_blk0 = q_blk0 + ATTN_HEADS
    v_blk0 = k_blk0 + ATTN_KV_HEADS
    qtab = pl.BlockSpec((tq, HEAD_DIM), lambda b, kv, g, i: (i, 0))
    ktab = pl.BlockSpec((S, HEAD_DIM), lambda b, kv, g, i: (0, 0))
    gain = pl.BlockSpec((1, HEAD_DIM), lambda b, kv, g, i: (0, 0))
    return pl.pallas_call(
        _attn_kernel,
        grid=(B, ATTN_KV_HEADS, ATTN_GROUP, S // tq),
        in_specs=[pl.BlockSpec((1, tq, HEAD_DIM),
                               lambda b, kv, g, i: (b, i, q_blk0 + kv * ATTN_GROUP + g)),
                  pl.BlockSpec((1, S, HEAD_DIM), lambda b, kv, g, i: (b, 0, k_blk0 + kv)),
                  pl.BlockSpec((1, S, HEAD_DIM), lambda b, kv, g, i: (b, 0, v_blk0 + kv)),
                  qtab, qtab, qtab, ktab, ktab, ktab, gain, gain],
        out_specs=pl.BlockSpec((1, tq, HEAD_DIM),
                               lambda b, kv, g, i: (b, i, kv * ATTN_GROUP + g)),
        out_shape=jax.ShapeDtypeStruct((B, S, ATTN_Q_W), BF16),
        scratch_shapes=[pltpu.VMEM((S, HEAD_DIM), BF16)],
        compiler_params=_params(("parallel", "parallel", "arbitrary", "arbitrary")),
        name="attention",
    )(z3, z3, z3, cos_t, slo_t, shi_t, cos_t, slo_t, shi_t, q_g, k_g)


def _merge_kernel(ym_ref, ya_ref, gm_ref, ga_ref, wm_ref, wa_ref, o_ref):
    bm = jnp.dot(ym_ref[...], wm_ref[...], preferred_element_type=F32)
    ba = jnp.dot(ya_ref[...], wa_ref[...], preferred_element_type=F32)
    o_ref[...] = (jax.nn.sigmoid(gm_ref[...].astype(F32)) * bm
                  + jax.nn.sigmoid(ga_ref[...].astype(F32)) * ba).astype(o_ref.dtype)


def _merge(y_m, y_a, z, gm_col0, ga_col0, w_m, w_a, tm, tn):
    T, K = y_m.shape
    D = w_m.shape[1]
    tm, tn = min(tm, T), min(tn, D)
    gm0, ga0 = gm_col0 // tn, ga_col0 // tn
    assert gm0 * tn == gm_col0 and ga0 * tn == ga_col0
    return pl.pallas_call(
        _merge_kernel,
        grid=(T // tm, D // tn),
        in_specs=[pl.BlockSpec((tm, K), lambda i, j: (i, 0)),
                  pl.BlockSpec((tm, K), lambda i, j: (i, 0)),
                  pl.BlockSpec((tm, tn), lambda i, j: (i, gm0 + j)),
                  pl.BlockSpec((tm, tn), lambda i, j: (i, ga0 + j)),
                  pl.BlockSpec((K, tn), lambda i, j: (0, j)),
                  pl.BlockSpec((K, tn), lambda i, j: (0, j))],
        out_specs=pl.BlockSpec((tm, tn), lambda i, j: (i, j)),
        out_shape=jax.ShapeDtypeStruct((T, D), BF16),
        compiler_params=_params(("parallel", "arbitrary")),
        name="merge",
    )(y_m, y_a, z, z, w_m, w_a)


def _outproj_kernel(a_ref, w_ref, x_ref, o_ref):
    o_ref[...] = x_ref[...] + jnp.dot(a_ref[...], w_ref[...], preferred_element_type=F32)


def _outproj(a, w, x, tm, tn):
    T, K = a.shape
    D = w.shape[1]
    tm, tn = min(tm, T), min(tn, D)
    return pl.pallas_call(
        _outproj_kernel,
        grid=(T // tm, D // tn),
        in_specs=[pl.BlockSpec((tm, K), lambda i, j: (i, 0)),
                  pl.BlockSpec((K, tn), lambda i, j: (0, j)),
                  pl.BlockSpec((tm, tn), lambda i, j: (i, j))],
        out_specs=pl.BlockSpec((tm, tn), lambda i, j: (i, j)),
        out_shape=jax.ShapeDtypeStruct((T, D), F32),
        compiler_params=_params(("parallel", "arbitrary")),
        name="outproj",
    )(a, w, x)


def _first_argmax(vals, vmax):
    idx = jnp.full(vals[0].shape, len(vals), jnp.int32)
    for j in reversed(range(len(vals))):
        idx = jnp.where(vals[j] == vmax, j, idx)
    return idx


def _router_kernel(x_ref, g_ref, wr_ref, br_ref, idx_ref, wts_ref, cnt_ref, tri_scr, carry_scr):
    tm = x_ref.shape[0]

    @pl.when(pl.program_id(0) == 0)
    def _():
        r = lax.broadcasted_iota(jnp.int32, (tm, tm), 0)
        c = lax.broadcasted_iota(jnp.int32, (tm, tm), 1)
        tri_scr[...] = jnp.where(r < c, 1.0, 0.0).astype(BF16)
        carry_scr[...] = jnp.zeros_like(carry_scr)

    h = _rms(x_ref[...], g_ref[...])
    logits = lax.dot_general(wr_ref[...], h, (((1,), (1,)), ((), ())),
                             preferred_element_type=F32,
                             precision=lax.Precision.HIGHEST) + br_ref[...]
    gl = [logits[j:j + 1, :] for j in range(N_GROUPS)]
    gmax = functools.reduce(jnp.maximum, gl)
    gi = _first_argmax(gl, gmax)
    gp = 1.0 / functools.reduce(jnp.add, [jnp.exp(v - gmax) for v in gl])
    eg = []
    for e in range(EXPERTS_PER_GROUP):
        v = logits[ROUTER_EXPERT_ROW0 + e:ROUTER_EXPERT_ROW0 + e + 1, :]
        for grp in range(1, N_GROUPS):
            r0 = ROUTER_EXPERT_ROW0 + grp * EXPERTS_PER_GROUP + e
            v = jnp.where(gi == grp, logits[r0:r0 + 1, :], v)
        eg.append(v)
    v1 = functools.reduce(jnp.maximum, eg)
    i1 = _first_argmax(eg, v1)
    eg2 = [jnp.where(i1 == e, -jnp.inf, eg[e]) for e in range(EXPERTS_PER_GROUP)]
    v2 = functools.reduce(jnp.maximum, eg2)
    i2 = _first_argmax(eg2, v2)
    e2 = jnp.exp(v2 - v1)
    w1 = gp / (1.0 + e2)
    w2 = gp * e2 / (1.0 + e2)
    eid1 = gi * EXPERTS_PER_GROUP + i1
    eid2 = gi * EXPERTS_PER_GROUP + i2

    erow = lax.broadcasted_iota(jnp.int32, (N_EXPERTS, tm), 0)
    oh1 = erow == eid1
    oh2 = erow == eid2
    oh = jnp.where(oh1 | oh2, 1.0, 0.0)
    before = jnp.dot(oh.astype(BF16), tri_scr[...], preferred_element_type=F32) + carry_scr[...]
    rank1 = jnp.sum(jnp.where(oh1, before, 0.0), axis=0, keepdims=True)
    rank2 = jnp.sum(jnp.where(oh2, before, 0.0), axis=0, keepdims=True)
    carry_scr[...] = carry_scr[...] + jnp.sum(oh, axis=1, keepdims=True)

    idx_ref[...] = jnp.zeros_like(idx_ref)
    idx_ref[0:1, :] = eid1
    idx_ref[1:2, :] = eid2
    idx_ref[2:3, :] = rank1.astype(jnp.int32)
    idx_ref[3:4, :] = rank2.astype(jnp.int32)
    wts_ref[...] = jnp.zeros_like(wts_ref)
    wts_ref[0:1, :] = w1
    wts_ref[1:2, :] = w2
    cnt_ref[...] = jnp.broadcast_to(carry_scr[...], cnt_ref.shape).astype(jnp.int32)


def _router(x, g, wr, br, tm):
    T, D = x.shape
    tm = min(tm, T)
    return pl.pallas_call(
        _router_kernel,
        grid=(T // tm,),
        in_specs=[pl.BlockSpec((tm, D), lambda i: (i, 0)),
                  pl.BlockSpec((1, D), lambda i: (0, 0)),
                  pl.BlockSpec((ROUTER_ROWS, D), lambda i: (0, 0)),
                  pl.BlockSpec((ROUTER_ROWS, 1), lambda i: (0, 0))],
        out_specs=[pl.BlockSpec((8, tm), lambda i: (0, i)),
                   pl.BlockSpec((8, tm), lambda i: (0, i)),
                   pl.BlockSpec((N_EXPERTS, 128), lambda i: (0, 0))],
        out_shape=[jax.ShapeDtypeStruct((8, T), jnp.int32),
                   jax.ShapeDtypeStruct((8, T), F32),
                   jax.ShapeDtypeStruct((N_EXPERTS, 128), jnp.int32)],
        scratch_shapes=[pltpu.VMEM((tm, tm), BF16), pltpu.VMEM((N_EXPERTS, 1), F32)],
        compiler_params=_params(("arbitrary",)),
        name="router",
    )(x, g, wr, br)


def _dispatch_kernel(dest_ref, x_hbm, xs_in, xs_hbm, sem, *, chunk, n_tok):
    del xs_in
    base = pl.program_id(0) * chunk

    def row_copy(r):
        a = base + r
        t = a - jnp.where(a >= n_tok, n_tok, 0)
        return pltpu.make_async_copy(x_hbm.at[pl.ds(t, 1)], xs_hbm.at[pl.ds(dest_ref[a], 1)], sem)

    def start(r, carry):
        row_copy(r).start()
        return carry

    def wait(r, carry):
        row_copy(r).wait()
        return carry

    lax.fori_loop(0, chunk, start, 0)
    lax.fori_loop(0, chunk, wait, 0)


def _dispatch(dest_flat, x, xs_zero, chunk):
    T, D = x.shape
    n = dest_flat.shape[0]
    chunk = min(chunk, n)
    return pl.pallas_call(
        functools.partial(_dispatch_kernel, chunk=chunk, n_tok=T),
        grid_spec=pltpu.PrefetchScalarGridSpec(
            num_scalar_prefetch=1,
            grid=(n // chunk,),
            in_specs=[pl.BlockSpec(memory_space=pl.ANY), pl.BlockSpec(memory_space=pl.ANY)],
            out_specs=pl.BlockSpec(memory_space=pl.ANY),
            scratch_shapes=[pltpu.SemaphoreType.DMA]),
        out_shape=jax.ShapeDtypeStruct(xs_zero.shape, xs_zero.dtype),
        input_output_aliases={2: 0},
        compiler_params=_params(("arbitrary",)),
        name="dispatch",
    )(dest_flat, x, xs_zero)


def _expert_kernel(be_ref, xs_ref, g_ref, wg_ref, wu_ref, wd_ref, ys_ref):
    del be_ref
    h = _rms(xs_ref[...], g_ref[...]).astype(BF16)
    a = jnp.dot(h, wg_ref[0], preferred_element_type=F32)
    u = jnp.dot(h, wu_ref[0], preferred_element_type=F32)
    act = (a * jax.nn.sigmoid(a) * u).astype(BF16)
    ys_ref[...] = jnp.dot(act, wd_ref[0], preferred_element_type=F32)


def _experts(blk_e, xs, g, w_g, w_u, w_d):
    P, D = xs.shape
    FF = w_g.shape[2]
    return pl.pallas_call(
        _expert_kernel,
        grid_spec=pltpu.PrefetchScalarGridSpec(
            num_scalar_prefetch=1,
            grid=(P // EXPERT_BLOCK,),
            in_specs=[pl.BlockSpec((EXPERT_BLOCK, D), lambda i, be: (i, 0)),
                      pl.BlockSpec((1, D), lambda i, be: (0, 0)),
                      pl.BlockSpec((1, D, FF), lambda i, be: (be[i], 0, 0)),
                      pl.BlockSpec((1, D, FF), lambda i, be: (be[i], 0, 0)),
                      pl.BlockSpec((1, FF, D), lambda i, be: (be[i], 0, 0))],
            out_specs=pl.BlockSpec((EXPERT_BLOCK, D), lambda i, be: (i, 0))),
        out_shape=jax.ShapeDtypeStruct((P, D), F32),
        compiler_params=_params(("arbitrary",)),
        name="experts",
    )(blk_e, xs, g, w_g, w_u, w_d)


def _combine_kernel(dest_ref, x_ref, w_ref, ys_hbm, o_ref, ybuf, sem, *, n_tok):
    tm = x_ref.shape[0]
    base = pl.program_id(0) * tm

    def row_copy(r, k):
        d = dest_ref[k * n_tok + base + r]
        return pltpu.make_async_copy(ys_hbm.at[pl.ds(d, 1)], ybuf.at[k, pl.ds(r, 1)], sem)

    def start(r, carry):
        row_copy(r, 0).start()
        row_copy(r, 1).start()
        return carry

    def wait(r, carry):
        row_copy(r, 0).wait()
        row_copy(r, 1).wait()
        return carry

    lax.fori_loop(0, tm, start, 0)
    lax.fori_loop(0, tm, wait, 0)
    w = w_ref[...]
    o_ref[...] = x_ref[...] + w[:, 0:1] * ybuf[0] + w[:, 1:2] * ybuf[1]


def _combine(dest_flat, x, w_cols, ys, tm):
    T, D = x.shape
    tm = min(tm, T)
    return pl.pallas_call(
        functools.partial(_combine_kernel, n_tok=T),
        grid_spec=pltpu.PrefetchScalarGridSpec(
            num_scalar_prefetch=1,
            grid=(T // tm,),
            in_specs=[pl.BlockSpec((tm, D), lambda i, d: (i, 0)),
                      pl.BlockSpec((tm, 2), lambda i, d: (i, 0)),
                      pl.BlockSpec(memory_space=pl.ANY)],
            out_specs=pl.BlockSpec((tm, D), lambda i, d: (i, 0)),
            scratch_shapes=[pltpu.VMEM((2, tm, D), F32), pltpu.SemaphoreType.DMA]),
        out_shape=jax.ShapeDtypeStruct((T, D), F32),
        compiler_params=_params(("arbitrary",)),
        name="combine",
    )(dest_flat, x, w_cols, ys)


def _ple_kernel(x_ref, g_ref, p_ref, wg_ref, wp_ref, o_ref, h_scr):
    tn = o_ref.shape[1]
    j = pl.program_id(1)

    @pl.when(j == 0)
    def _():
        _norm_rows_into(x_ref, g_ref, h_scr)

    gate = jax.nn.sigmoid(jnp.dot(h_scr[...], wg_ref[...], preferred_element_type=F32))
    proj = jnp.dot(p_ref[...].astype(BF16), wp_ref[...], preferred_element_type=F32)
    o_ref[...] = x_ref[:, pl.ds(pl.multiple_of(j * tn, tn), tn)] + gate * proj


def _ple(x, g, p, w_g, w_p, tm, tn):
    T, D = x.shape
    PD = p.shape[1]
    tm, tn = min(tm, T), min(tn, D)
    return pl.pallas_call(
        _ple_kernel,
        grid=(T // tm, D // tn),
        in_specs=[pl.BlockSpec((tm, D), lambda i, j: (i, 0)),
                  pl.BlockSpec((1, D), lambda i, j: (0, 0)),
                  pl.BlockSpec((tm, PD), lambda i, j: (i, 0)),
                  pl.BlockSpec((D, tn), lambda i, j: (0, j)),
                  pl.BlockSpec((PD, tn), lambda i, j: (0, j))],
        out_specs=pl.BlockSpec((tm, tn), lambda i, j: (i, j)),
        out_shape=jax.ShapeDtypeStruct((T, D), F32),
        scratch_shapes=[pltpu.VMEM((tm, D), BF16)],
        compiler_params=_params(("parallel", "arbitrary")),
        name="ple",
    )(x, g, p, w_g, w_p)


def _final_norm_kernel(x_ref, g_ref, o_ref):
    o_ref[...] = _rms(x_ref[...], g_ref[...])


def _final_norm(x, g, tm):
    T, D = x.shape
    tm = min(tm, T)
    return pl.pallas_call(
        _final_norm_kernel,
        grid=(T // tm,),
        in_specs=[pl.BlockSpec((tm, D), lambda i: (i, 0)), pl.BlockSpec((1, D), lambda i: (0, 0))],
        out_specs=pl.BlockSpec((tm, D), lambda i: (i, 0)),
        out_shape=jax.ShapeDtypeStruct((T, D), F32),
        compiler_params=_params(("parallel",)),
        name="final_norm",
    )(x, g)


def _rope_tables(S):
    pos = np.arange(S)
    inv_freq = ROPE_THETA ** (-np.arange(ROPE_PAIRS, dtype=np.float32) / ROPE_PAIRS)
    ang_row = jnp.asarray((pos // GRID_W).astype(np.float32)[:, None] * inv_freq[None, :])
    ang_col = jnp.asarray((pos % GRID_W).astype(np.float32)[:, None] * inv_freq[None, :])
    zero = jnp.zeros_like(ang_row)
    cos_t = jnp.concatenate([jnp.cos(ang_row)] * 2 + [jnp.cos(ang_col)] * 2, axis=1)
    sin_lo = jnp.concatenate([-jnp.sin(ang_row), zero, -jnp.sin(ang_col), zero], axis=1)
    sin_hi = jnp.concatenate([zero, jnp.sin(ang_row), zero, jnp.sin(ang_col)], axis=1)
    return cos_t, sin_lo, sin_hi


def _moe_layout(idx, counts, T):
    padded = ((counts + EXPERT_BLOCK - 1) // EXPERT_BLOCK) * EXPERT_BLOCK
    pad_end = jnp.cumsum(padded)
    pad_start = pad_end - padded
    dest = pad_start[idx[0:2]] + idx[2:4]
    n_blk = (2 * T + N_EXPERTS * EXPERT_BLOCK) // EXPERT_BLOCK
    blk_e = jnp.minimum(jnp.searchsorted(pad_end, jnp.arange(n_blk, dtype=jnp.int32) * EXPERT_BLOCK,
                                         side="right"), N_EXPERTS - 1).astype(jnp.int32)
    return dest.reshape(-1).astype(jnp.int32), blk_e, n_blk


def kernel(x, p, norm1_g, w_in, mlstm_gate_b, mlstm_norm_g, q_norm_g, k_norm_g, w_branch_m, w_branch_a, w_out, norm2_g, w_router_group, b_router_group, w_router_expert, b_router_expert, w_exp_gate, w_exp_up, w_exp_down, ple_norm_g, w_ple_gate, w_ple_proj, final_norm_g):
    B, S, D = x.shape
    depth = w_in.shape[0]
    T = B * S
    H = MLSTM_HEADS
    nc = S // MLSTM_CHUNK
    scale = HEAD_DIM ** -0.5
    tables = _rope_tables(S)
    row = lambda v: v.reshape(1, -1).astype(F32)

    o_qm, o_gates = 0, 4 * MLSTM_W
    o_qa = o_gates + 4 * H
    o_gm = o_qa + ATTN_Q_W + 2 * ATTN_KV_W
    o_ga = o_gm + D
    z_qa = 4 * MLSTM_W
    z_gm = z_qa + ATTN_Q_W + 2 * ATTN_KV_W
    z_ga = z_gm + D

    xf = x.reshape(T, D)
    for i in range(depth):
        wi = w_in[i]
        w_main = jnp.concatenate([wi[:, o_qm:o_qm + MLSTM_W] * scale, wi[:, MLSTM_W:o_gates],
                                  wi[:, o_qa:]], axis=1).astype(BF16)
        w_gates = jnp.pad(wi[:, o_gates:o_qa], ((0, 0), (0, GATE_LANES - 4 * H))).astype(BF16)
        z, gates = _inproj(xf, row(norm1_g[i]), w_main, w_gates, 1024, 512)
        z3 = z.reshape(B, S, -1)

        gr = gates[:, :4 * H].reshape(B, nc, MLSTM_CHUNK, 4, H).transpose(0, 4, 1, 3, 2)
        gr = jnp.pad(gr, ((0, 0), (0, 0), (0, 0), (0, 4), (0, 0)))
        gb = jnp.pad(mlstm_gate_b[i].astype(F32).reshape(4, H).T, ((0, 0), (0, 4)))
        gb = jnp.broadcast_to(gb[:, :, None], (H, 8, MLSTM_CHUNK))
        y_m = _mlstm(z3, gr, gb, row(mlstm_norm_g[i]))
        y_a = _attention(z3, z_qa // HEAD_DIM, tables, row(q_norm_g[i]) * scale, row(k_norm_g[i]), 512)

        merged = _merge(y_m.reshape(T, MLSTM_W), y_a.reshape(T, ATTN_Q_W), z, z_gm, z_ga,
                        w_branch_m[i].astype(BF16), w_branch_a[i].astype(BF16), 1024, 512)
        xf = _outproj(merged, w_out[i].astype(BF16), xf, 1024, 512)

        wr = jnp.zeros((ROUTER_ROWS, D), F32)
        wr = wr.at[:N_GROUPS].set(w_router_group[i].T)
        wr = wr.at[ROUTER_EXPERT_ROW0:ROUTER_EXPERT_ROW0 + N_EXPERTS].set(w_router_expert[i].T)
        br = jnp.zeros((ROUTER_ROWS, 1), F32)
        br = br.at[:N_GROUPS, 0].set(b_router_group[i])
        br = br.at[ROUTER_EXPERT_ROW0:ROUTER_EXPERT_ROW0 + N_EXPERTS, 0].set(b_router_expert[i])
        idx, wts, cnt = _router(xf, row(norm2_g[i]), wr, br, 512)
        dest, blk_e, n_blk = _moe_layout(idx, cnt[:, 0], T)
        xs = _dispatch(dest, xf, jnp.zeros((n_blk * EXPERT_BLOCK, D), F32), 512)
        ys = _experts(blk_e, xs, row(norm2_g[i]), w_exp_gate[i].astype(BF16),
                      w_exp_up[i].astype(BF16), w_exp_down[i].astype(BF16))
        xf = _combine(dest, xf, wts[0:2].T, ys, 256)

        xf = _ple(xf, row(ple_norm_g[i]), p[i].reshape(T, -1), w_ple_gate[i].astype(BF16),
                  w_ple_proj[i].astype(BF16), 1024, 512)

    return _final_norm(xf, row(final_norm_g), 512).reshape(B, S, D)
```

```python
import functools

import jax
import jax.numpy as jnp
import numpy as np
from jax import lax
from jax.experimental import pallas as pl
from jax.experimental.pallas import tpu as pltpu

F32 = jnp.float32
BF16 = jnp.bfloat16

GRID_W = 64
HEAD_DIM = 128
MLSTM_HEADS = 8
MLSTM_W = MLSTM_HEADS * HEAD_DIM
MLSTM_CHUNK = 128
ATTN_HEADS = 8
ATTN_KV_HEADS = 2
ATTN_GROUP = ATTN_HEADS // ATTN_KV_HEADS
ATTN_Q_W = ATTN_HEADS * HEAD_DIM
ATTN_KV_W = ATTN_KV_HEADS * HEAD_DIM
ROPE_THETA = 10000.0
ROPE_PAIRS = HEAD_DIM // 4
N_GROUPS = 4
EXPERTS_PER_GROUP = 8
N_EXPERTS = N_GROUPS * EXPERTS_PER_GROUP
EXPERT_BLOCK = 128
EPS = 1e-6
GATE_LANES = 128
ROUTER_ROWS = 128
ROUTER_EXPERT_ROW0 = 8
VMEM_LIMIT_BYTES = 52 * 1024 * 1024
NORM_ROWS = 128


def _params(semantics, **kw):
    return pltpu.CompilerParams(dimension_semantics=semantics,
                                vmem_limit_bytes=VMEM_LIMIT_BYTES, **kw)


def _rms(x, g):
    ms = jnp.mean(x * x, axis=-1, keepdims=True)
    return (x * lax.rsqrt(ms + EPS)) * g


def _norm_rows_into(x_ref, g_ref, h_scr):
    rows = x_ref.shape[0]
    step = min(NORM_ROWS, rows)

    def body(c, carry):
        r = pl.ds(pl.multiple_of(c * step, step), step)
        h_scr[r, :] = _rms(x_ref[r, :], g_ref[...]).astype(h_scr.dtype)
        return carry

    lax.fori_loop(0, rows // step, body, 0)


def _inproj_kernel(x_ref, g_ref, w_ref, wg_ref, z_ref, gates_ref, h_scr):
    @pl.when(pl.program_id(1) == 0)
    def _():
        _norm_rows_into(x_ref, g_ref, h_scr)
        gates_ref[...] = jnp.dot(h_scr[...], wg_ref[...], preferred_element_type=F32)

    z_ref[...] = jnp.dot(h_scr[...], w_ref[...], preferred_element_type=F32).astype(z_ref.dtype)


def _inproj(x, g, w, wg, tm, tn):
    T, D = x.shape
    N = w.shape[1]
    tm, tn = min(tm, T), min(tn, N)
    return pl.pallas_call(
        _inproj_kernel,
        grid=(T // tm, N // tn),
        in_specs=[pl.BlockSpec((tm, D), lambda i, j: (i, 0)),
                  pl.BlockSpec((1, D), lambda i, j: (0, 0)),
                  pl.BlockSpec((D, tn), lambda i, j: (0, j)),
                  pl.BlockSpec((D, GATE_LANES), lambda i, j: (0, 0))],
        out_specs=[pl.BlockSpec((tm, tn), lambda i, j: (i, j)),
                   pl.BlockSpec((tm, GATE_LANES), lambda i, j: (i, 0))],
        out_shape=[jax.ShapeDtypeStruct((T, N), BF16),
                   jax.ShapeDtypeStruct((T, GATE_LANES), F32)],
        scratch_shapes=[pltpu.VMEM((tm, D), BF16)],
        compiler_params=_params(("parallel", "arbitrary")),
        name="inproj",
    )(x, g, w, wg)


def _log_sigmoid(x):
    return jnp.minimum(x, 0.0) - jnp.log1p(jnp.exp(-jnp.abs(x)))


def _mlstm_kernel(q_ref, k_ref, v_ref, o_ref, gr_ref, gb_ref, ng_ref, y_ref,
                  hf_scr, hb_scr, c_scr):
    L = MLSTM_CHUNK
    S = q_ref.shape[1]
    nc = S // L
    lane = lax.broadcasted_iota(jnp.int32, (8, L), 1)
    sub = lax.broadcasted_iota(jnp.int32, (8, L), 0)
    tt = lax.broadcasted_iota(jnp.int32, (L, L), 0)
    ss = lax.broadcasted_iota(jnp.int32, (L, L), 1)
    mask_fw = ss <= tt
    mask_bw = ss >= tt
    c_scr[...] = jnp.zeros_like(c_scr)

    def gate_rows(c):
        g = gr_ref[0, 0, c] + gb_ref[0]
        ls = _log_sigmoid(g)
        cf = ls
        cb = ls
        k = 1
        while k < L:
            cf = cf + jnp.where(lane >= k, pltpu.roll(cf, k, 1), 0.0)
            cb = cb + jnp.where(lane < L - k, pltpu.roll(cb, L - k, 1), 0.0)
            k *= 2
        bfw = jnp.broadcast_to(cf[2:3], (8, L))
        bbw = jnp.broadcast_to(cb[3:4], (8, L))
        ifw = jnp.broadcast_to(g[0:1], (8, L))
        ibw = jnp.broadcast_to(g[1:2], (8, L))
        return jnp.where(sub == 0, bfw,
               jnp.where(sub == 1, ifw - bfw,
               jnp.where(sub == 2, bbw,
               jnp.where(sub == 3, ibw - bbw, 0.0))))

    def direction(c, rows, cols, r0, mask, last, d, n, m, h_scr):
        sl = pl.ds(pl.multiple_of(c * L, L), L)
        qi = q_ref[0, sl, :]
        ki = k_ref[0, sl, :]
        vi = v_ref[0, sl, :]
        bcol = cols[:, r0:r0 + 1]
        ucol = cols[:, r0 + 1:r0 + 2]
        urow = rows[r0 + 1:r0 + 2, :]
        b_last = bcol[last:last + 1, :]
        dm = jnp.where(mask, bcol + urow, -jnp.inf)
        inter = bcol + m
        m_t = jnp.maximum(jnp.max(dm, axis=1, keepdims=True), inter)
        s = lax.dot_general(qi, ki, (((1,), (1,)), ((), ())),
                            preferred_element_type=F32) * jnp.exp(dm - m_t)
        gq = jnp.exp(inter - m_t)
        C = c_scr[d]
        num = (jnp.dot(s.astype(BF16), vi, preferred_element_type=F32)
               + gq * jnp.dot(qi, C.astype(BF16), preferred_element_type=F32))
        qn = jnp.sum(qi.astype(F32) * n, axis=1, keepdims=True)
        den = jnp.sum(s, axis=1, keepdims=True) + gq * qn
        h_scr[sl, :] = num / jnp.maximum(jnp.abs(den), jnp.exp(-m_t))
        a = b_last + ucol
        m_new = jnp.maximum(b_last + m, jnp.max(a, axis=0, keepdims=True))
        decay = jnp.exp(b_last + m - m_new)
        kw = jnp.exp(a - m_new) * ki.astype(F32)
        c_scr[d] = decay * C + jnp.dot(kw.T.astype(BF16), vi, preferred_element_type=F32)
        n_new = decay * n + jnp.sum(kw, axis=0, keepdims=True)
        return n_new, m_new

    def body(j, carry):
        n_f, m_f, n_b, m_b = carry
        cf = j
        cb = nc - 1 - j
        rows_f = gate_rows(cf)
        rows_b = gate_rows(cb)
        pad = jnp.zeros((L - 8, L), F32)
        cols_f = jnp.concatenate([rows_f, pad], axis=0).T
        cols_b = jnp.concatenate([rows_b, pad], axis=0).T
        n_f, m_f = direction(cf, rows_f, cols_f, 0, mask_fw, L - 1, 0, n_f, m_f, hf_scr)
        n_b, m_b = direction(cb, rows_b, cols_b, 2, mask_bw, 0, 1, n_b, m_b, hb_scr)
        return n_f, m_f, n_b, m_b

    zn = jnp.zeros((1, HEAD_DIM), F32)
    zm = jnp.zeros((1, 1), F32)
    lax.fori_loop(0, nc, body, (zn, zm, zn, zm))

    def finish(c, carry):
        sl = pl.ds(pl.multiple_of(c * L, L), L)
        hn = _rms(hf_scr[sl, :] + hb_scr[sl, :], ng_ref[...])
        o = o_ref[0, sl, :].astype(F32)
        y_ref[0, sl, :] = (hn * jax.nn.sigmoid(o)).astype(y_ref.dtype)
        return carry

    lax.fori_loop(0, nc, finish, 0)


def _mlstm(z3, gate_rows, gate_bias_rows, norm_g):
    B, S, _ = z3.shape
    H = MLSTM_HEADS
    nc = S // MLSTM_CHUNK
    col = lambda off: pl.BlockSpec((1, S, HEAD_DIM), lambda b, h: (b, 0, off + h))
    return pl.pallas_call(
        _mlstm_kernel,
        grid=(B, H),
        in_specs=[col(0), col(H), col(2 * H), col(3 * H),
                  pl.BlockSpec((1, 1, nc, 8, MLSTM_CHUNK), lambda b, h: (b, h, 0, 0, 0)),
                  pl.BlockSpec((1, 8, MLSTM_CHUNK), lambda b, h: (h, 0, 0)),
                  pl.BlockSpec((1, HEAD_DIM), lambda b, h: (0, h))],
        out_specs=pl.BlockSpec((1, S, HEAD_DIM), lambda b, h: (b, 0, h)),
        out_shape=jax.ShapeDtypeStruct((B, S, MLSTM_W), BF16),
        scratch_shapes=[pltpu.VMEM((S, HEAD_DIM), F32), pltpu.VMEM((S, HEAD_DIM), F32),
                        pltpu.VMEM((2, HEAD_DIM, HEAD_DIM), F32)],
        compiler_params=_params(("parallel", "arbitrary")),
        name="mlstm",
    )(z3, z3, z3, z3, gate_rows, gate_bias_rows, norm_g)


def _rope(t, c, s_lo, s_hi):
    quarter = HEAD_DIM // 4
    return (t * c + pltpu.roll(t, HEAD_DIM - quarter, 1) * s_lo
            + pltpu.roll(t, quarter, 1) * s_hi)


def _attn_kernel(q_ref, k_ref, v_ref, cq_ref, sloq_ref, shiq_ref, ck_ref, slok_ref, shik_ref,
                 qg_ref, kg_ref, o_ref, k_scr):
    S = k_ref.shape[1]
    step = min(256, S)

    @pl.when((pl.program_id(2) == 0) & (pl.program_id(3) == 0))
    def _():
        def body(c, carry):
            r = pl.ds(pl.multiple_of(c * step, step), step)
            kn = _rms(k_ref[0, r, :].astype(F32), kg_ref[...])
            k_scr[r, :] = _rope(kn, ck_ref[r, :], slok_ref[r, :], shik_ref[r, :]).astype(BF16)
            return carry

        lax.fori_loop(0, S // step, body, 0)

    qn = _rms(q_ref[0].astype(F32), qg_ref[...])
    qr = _rope(qn, cq_ref[...], sloq_ref[...], shiq_ref[...]).astype(BF16)
    s = lax.dot_general(qr, k_scr[...], (((1,), (1,)), ((), ())), preferred_element_type=F32)
    p = jnp.exp(s - jnp.max(s, axis=1, keepdims=True))
    l = jnp.sum(p, axis=1, keepdims=True)
    o = jnp.dot(p.astype(BF16), v_ref[0], preferred_element_type=F32)
    o_ref[0] = (o / l).astype(o_ref.dtype)


def _attention(z3, q_blk0, tables, q_g, k_g, tq):
    B, S, _ = z3.shape
    tq = min(tq, S)
    cos_t, slo_t, shi_t = tables
    k_blk0 = q_blk0 + ATTN_HEADS
    v_blk0 = k_blk0 + ATTN_KV_HEADS
    qtab = pl.BlockSpec((tq, HEAD_DIM), lambda b, kv, g, i: (i, 0))
    ktab = pl.BlockSpec((S, HEAD_DIM), lambda b, kv, g, i: (0, 0))
    gain = pl.BlockSpec((1, HEAD_DIM), lambda b, kv, g, i: (0, 0))
    return pl.pallas_call(
        _attn_kernel,
        grid=(B, ATTN_KV_HEADS, ATTN_GROUP, S // tq),
        in_specs=[pl.BlockSpec((1, tq, HEAD_DIM),
                               lambda b, kv, g, i: (b, i, q_blk0 + kv * ATTN_GROUP + g)),
                  pl.BlockSpec((1, S, HEAD_DIM), lambda b, kv, g, i: (b, 0, k_blk0 + kv)),
                  pl.BlockSpec((1, S, HEAD_DIM), lambda b, kv, g, i: (b, 0, v_blk0 + kv)),
                  qtab, qtab, qtab, ktab, ktab, ktab, gain, gain],
        out_specs=pl.BlockSpec((1, tq, HEAD_DIM),
                               lambda b, kv, g, i: (b, i, kv * ATTN_GROUP + g)),
        out_shape=jax.ShapeDtypeStruct((B, S, ATTN_Q_W), BF16),
        scratch_shapes=[pltpu.VMEM((S, HEAD_DIM), BF16)],
        compiler_params=_params(("parallel", "parallel", "arbitrary", "arbitrary")),
        name="attention",
    )(z3, z3, z3, cos_t, slo_t, shi_t, cos_t, slo_t, shi_t, q_g, k_g)


def _merge_kernel(ym_ref, ya_ref, gm_ref, ga_ref, wm_ref, wa_ref, o_ref):
    bm = jnp.dot(ym_ref[...], wm_ref[...], preferred_element_type=F32)
    ba = jnp.dot(ya_ref[...], wa_ref[...], preferred_element_type=F32)
    o_ref[...] = (jax.nn.sigmoid(gm_ref[...].astype(F32)) * bm
                  + jax.nn.sigmoid(ga_ref[...].astype(F32)) * ba).astype(o_ref.dtype)


def _merge(y_m, y_a, z, gm_col0, ga_col0, w_m, w_a, tm, tn):
    T, K = y_m.shape
    D = w_m.shape[1]
    tm, tn = min(tm, T), min(tn, D)
    gm0, ga0 = gm_col0 // tn, ga_col0 // tn
    assert gm0 * tn == gm_col0 and ga0 * tn == ga_col0
    return pl.pallas_call(
        _merge_kernel,
        grid=(T // tm, D // tn),
        in_specs=[pl.BlockSpec((tm, K), lambda i, j: (i, 0)),
                  pl.BlockSpec((tm, K), lambda i, j: (i, 0)),
                  pl.BlockSpec((tm, tn), lambda i, j: (i, gm0 + j)),
                  pl.BlockSpec((tm, tn), lambda i, j: (i, ga0 + j)),
                  pl.BlockSpec((K, tn), lambda i, j: (0, j)),
                  pl.BlockSpec((K, tn), lambda i, j: (0, j))],
        out_specs=pl.BlockSpec((tm, tn), lambda i, j: (i, j)),
        out_shape=jax.ShapeDtypeStruct((T, D), BF16),
        compiler_params=_params(("parallel", "arbitrary")),
        name="merge",
    )(y_m, y_a, z, z, w_m, w_a)


def _outproj_kernel(a_ref, w_ref, x_ref, o_ref):
    o_ref[...] = x_ref[...] + jnp.dot(a_ref[...], w_ref[...], preferred_element_type=F32)


def _outproj(a, w, x, tm, tn):
    T, K = a.shape
    D = w.shape[1]
    tm, tn = min(tm, T), min(tn, D)
    return pl.pallas_call(
        _outproj_kernel,
        grid=(T // tm, D // tn),
        in_specs=[pl.BlockSpec((tm, K), lambda i, j: (i, 0)),
                  pl.BlockSpec((K, tn), lambda i, j: (0, j)),
                  pl.BlockSpec((tm, tn), lambda i, j: (i, j))],
        out_specs=pl.BlockSpec((tm, tn), lambda i, j: (i, j)),
        out_shape=jax.ShapeDtypeStruct((T, D), F32),
        compiler_params=_params(("parallel", "arbitrary")),
        name="outproj",
    )(a, w, x)


def _first_argmax(vals, vmax):
    idx = jnp.full(vals[0].shape, len(vals), jnp.int32)
    for j in reversed(range(len(vals))):
        idx = jnp.where(vals[j] == vmax, j, idx)
    return idx


def _router_kernel(x_ref, g_ref, wr_ref, br_ref, idx_ref, wts_ref, cnt_ref, tri_scr, carry_scr):
    tm = x_ref.shape[0]

    @pl.when(pl.program_id(0) == 0)
    def _():
        r = lax.broadcasted_iota(jnp.int32, (tm, tm), 0)
        c = lax.broadcasted_iota(jnp.int32, (tm, tm), 1)
        tri_scr[...] = jnp.where(r < c, 1.0, 0.0).astype(BF16)
        carry_scr[...] = jnp.zeros_like(carry_scr)

    h = _rms(x_ref[...], g_ref[...])
    logits = lax.dot_general(wr_ref[...], h, (((1,), (1,)), ((), ())),
                             preferred_element_type=F32,
                             precision=lax.Precision.HIGHEST) + br_ref[...]
    gl = [logits[j:j + 1, :] for j in range(N_GROUPS)]
    gmax = functools.reduce(jnp.maximum, gl)
    gi = _first_argmax(gl, gmax)
    gp = 1.0 / functools.reduce(jnp.add, [jnp.exp(v - gmax) for v in gl])
    eg = []
    for e in range(EXPERTS_PER_GROUP):
        v = logits[ROUTER_EXPERT_ROW0 + e:ROUTER_EXPERT_ROW0 + e + 1, :]
        for grp in range(1, N_GROUPS):
            r0 = ROUTER_EXPERT_ROW0 + grp * EXPERTS_PER_GROUP + e
            v = jnp.where(gi == grp, logits[r0:r0 + 1, :], v)
        eg.append(v)
    v1 = functools.reduce(jnp.maximum, eg)
    i1 = _first_argmax(eg, v1)
    eg2 = [jnp.where(i1 == e, -jnp.inf, eg[e]) for e in range(EXPERTS_PER_GROUP)]
    v2 = functools.reduce(jnp.maximum, eg2)
    i2 = _first_argmax(eg2, v2)
    e2 = jnp.exp(v2 - v1)
    w1 = gp / (1.0 + e2)
    w2 = gp * e2 / (1.0 + e2)
    eid1 = gi * EXPERTS_PER_GROUP + i1
    eid2 = gi * EXPERTS_PER_GROUP + i2

    erow = lax.broadcasted_iota(jnp.int32, (N_EXPERTS, tm), 0)
    oh1 = erow == eid1
    oh2 = erow == eid2
    oh = jnp.where(oh1 | oh2, 1.0, 0.0)
    before = jnp.dot(oh.astype(BF16), tri_scr[...], preferred_element_type=F32) + carry_scr[...]
    rank1 = jnp.sum(jnp.where(oh1, before, 0.0), axis=0, keepdims=True)
    rank2 = jnp.sum(jnp.where(oh2, before, 0.0), axis=0, keepdims=True)
    carry_scr[...] = carry_scr[...] + jnp.sum(oh, axis=1, keepdims=True)

    idx_ref[...] = jnp.zeros_like(idx_ref)
    idx_ref[0:1, :] = eid1
    idx_ref[1:2, :] = eid2
    idx_ref[2:3, :] = rank1.astype(jnp.int32)
    idx_ref[3:4, :] = rank2.astype(jnp.int32)
    wts_ref[...] = jnp.zeros_like(wts_ref)
    wts_ref[0:1, :] = w1
    wts_ref[1:2, :] = w2
    cnt_ref[...] = jnp.broadcast_to(carry_scr[...], cnt_ref.shape).astype(jnp.int32)


def _router(x, g, wr, br, tm):
    T, D = x.shape
    tm = min(tm, T)
    return pl.pallas_call(
        _router_kernel,
        grid=(T // tm,),
        in_specs=[pl.BlockSpec((tm, D), lambda i: (i, 0)),
                  pl.BlockSpec((1, D), lambda i: (0, 0)),
                  pl.BlockSpec((ROUTER_ROWS, D), lambda i: (0, 0)),
                  pl.BlockSpec((ROUTER_ROWS, 1), lambda i: (0, 0))],
        out_specs=[pl.BlockSpec((8, tm), lambda i: (0, i)),
                   pl.BlockSpec((8, tm), lambda i: (0, i)),
                   pl.BlockSpec((N_EXPERTS, 128), lambda i: (0, 0))],
        out_shape=[jax.ShapeDtypeStruct((8, T), jnp.int32),
                   jax.ShapeDtypeStruct((8, T), F32),
                   jax.ShapeDtypeStruct((N_EXPERTS, 128), jnp.int32)],
        scratch_shapes=[pltpu.VMEM((tm, tm), BF16), pltpu.VMEM((N_EXPERTS, 1), F32)],
        compiler_params=_params(("arbitrary",)),
        name="router",
    )(x, g, wr, br)


def _slots_kernel(idx_ref, cnt_ref, dest_ref, blk_ref):
    shift = EXPERT_BLOCK.bit_length() - 1
    cnt = cnt_ref[...]
    padded = ((cnt + (EXPERT_BLOCK - 1)) >> shift) << shift
    eid1, eid2 = idx_ref[0:1, :], idx_ref[1:2, :]
    d1, d2 = idx_ref[2:3, :], idx_ref[3:4, :]
    blk_start = lax.broadcasted_iota(jnp.int32, (1, blk_ref.shape[1]), 1) * EXPERT_BLOCK
    blk_e = jnp.zeros_like(blk_start)
    acc = jnp.zeros((1, 1), jnp.int32)
    for e in range(N_EXPERTS):
        d1 = d1 + jnp.where(eid1 == e, acc, 0)
        d2 = d2 + jnp.where(eid2 == e, acc, 0)
        acc = acc + padded[e:e + 1, 0:1]
        blk_e = blk_e + jnp.where(acc <= blk_start, 1, 0)
    dest_ref[...] = jnp.zeros_like(dest_ref)
    dest_ref[0:1, :] = d1
    dest_ref[1:2, :] = d2
    blk_ref[...] = jnp.zeros_like(blk_ref)
    blk_ref[0:1, :] = jnp.minimum(blk_e, N_EXPERTS - 1)
    blk_ref[1:2, :] = jnp.broadcast_to(acc >> shift, blk_start.shape)


def _slots(idx, cnt, n_blk, tm):
    T = idx.shape[1]
    tm = min(tm, T)
    lanes = -(-n_blk // 128) * 128
    return pl.pallas_call(
        _slots_kernel,
        grid=(T // tm,),
        in_specs=[pl.BlockSpec((8, tm), lambda i: (0, i)),
                  pl.BlockSpec((N_EXPERTS, 128), lambda i: (0, 0))],
        out_specs=[pl.BlockSpec((8, tm), lambda i: (0, i)),
                   pl.BlockSpec((8, lanes), lambda i: (0, 0))],
        out_shape=[jax.ShapeDtypeStruct((8, T), jnp.int32),
                   jax.ShapeDtypeStruct((8, lanes), jnp.int32)],
        compiler_params=_params(("arbitrary",)),
        name="slots",
    )(idx, cnt)


def _invert_kernel(dest_ref, slot_ref, *, n_tok, slot_chunk, tok_chunk):
    phase, step = pl.program_id(0), pl.program_id(1)

    @pl.when(phase == 0)
    def _():
        def zero(s, carry):
            slot_ref[step * slot_chunk + s] = 0
            return carry
        lax.fori_loop(0, slot_chunk, zero, 0, unroll=8)

    @pl.when(phase == 1)
    def _():
        def put(r, carry):
            t = step * tok_chunk + r
            slot_ref[dest_ref[t]] = t
            slot_ref[dest_ref[n_tok + t]] = t
            return carry
        lax.fori_loop(0, tok_chunk, put, 0, unroll=8)


def _invert(dest_flat, n_slots, n_tok, steps):
    while n_slots % steps or n_tok % steps:
        steps //= 2
    return pl.pallas_call(
        functools.partial(_invert_kernel, n_tok=n_tok, slot_chunk=n_slots // steps,
                          tok_chunk=n_tok // steps),
        grid=(2, steps),
        in_specs=[pl.BlockSpec(memory_space=pltpu.SMEM)],
        out_specs=pl.BlockSpec(memory_space=pltpu.SMEM),
        out_shape=jax.ShapeDtypeStruct((n_slots,), jnp.int32),
        compiler_params=_params(("arbitrary", "arbitrary")),
        name="invert",
    )(dest_flat)


def _expert_kernel(be_ref, tok_ref, used_ref, x_hbm, g_ref, wg_ref, wu_ref, wd_ref, ys_ref,
                   xbuf, sem):
    del be_ref
    i = pl.program_id(0)
    n_used = used_ref[0]

    def row_copy(blk, buf, r):
        t = tok_ref[blk * EXPERT_BLOCK + r]
        return pltpu.make_async_copy(x_hbm.at[pl.ds(t, 1)], xbuf.at[buf, pl.ds(r, 1)], sem.at[buf])

    def start_block(blk):
        def body(r, carry):
            row_copy(blk, blk % 2, r).start()
            return carry
        lax.fori_loop(0, EXPERT_BLOCK, body, 0, unroll=8)

    @pl.when((i == 0) & (n_used > 0))
    def _():
        start_block(0)

    @pl.when(i + 1 < n_used)
    def _():
        start_block(i + 1)

    @pl.when(i < n_used)
    def _():
        def body(r, carry):
            row_copy(i, i % 2, r).wait()
            return carry
        lax.fori_loop(0, EXPERT_BLOCK, body, 0, unroll=8)
        h = _rms(xbuf[i % 2], g_ref[...]).astype(BF16)
        a = jnp.dot(h, wg_ref[0], preferred_element_type=F32)
        u = jnp.dot(h, wu_ref[0], preferred_element_type=F32)
        act = (a * jax.nn.sigmoid(a) * u).astype(BF16)
        ys_ref[...] = jnp.dot(act, wd_ref[0], preferred_element_type=F32)

    @pl.when(i >= n_used)
    def _():
        ys_ref[...] = jnp.zeros_like(ys_ref)


def _experts(blk_e, slot_tok, n_used, x, g, w_g, w_u, w_d):
    D = x.shape[1]
    FF = w_g.shape[2]
    n_blk = blk_e.shape[0]
    return pl.pallas_call(
        _expert_kernel,
        grid_spec=pltpu.PrefetchScalarGridSpec(
            num_scalar_prefetch=3,
            grid=(n_blk,),
            in_specs=[pl.BlockSpec(memory_space=pl.ANY),
                      pl.BlockSpec((1, D), lambda i, be, tok, nu: (0, 0)),
                      pl.BlockSpec((1, D, FF), lambda i, be, tok, nu: (be[i], 0, 0)),
                      pl.BlockSpec((1, D, FF), lambda i, be, tok, nu: (be[i], 0, 0)),
                      pl.BlockSpec((1, FF, D), lambda i, be, tok, nu: (be[i], 0, 0))],
            out_specs=pl.BlockSpec((EXPERT_BLOCK, D), lambda i, be, tok, nu: (i, 0)),
            scratch_shapes=[pltpu.VMEM((2, EXPERT_BLOCK, D), F32),
                            pltpu.SemaphoreType.DMA((2,))]),
        out_shape=jax.ShapeDtypeStruct((n_blk * EXPERT_BLOCK, D), F32),
        compiler_params=_params(("arbitrary",)),
        name="experts",
    )(blk_e, slot_tok, n_used, x, g, w_g, w_u, w_d)


def _combine_kernel(dest_ref, x_ref, w_ref, ys_hbm, o_ref, ybuf, sem, *, n_tok):
    tm = x_ref.shape[0]
    i = pl.program_id(0)

    def row_copy(step, r, k):
        d = dest_ref[k * n_tok + step * tm + r]
        buf = step % 2
        return pltpu.make_async_copy(ys_hbm.at[pl.ds(d, 1)], ybuf.at[buf, k, pl.ds(r, 1)],
                                     sem.at[buf])

    def start_step(step):
        def body(r, carry):
            row_copy(step, r, 0).start()
            row_copy(step, r, 1).start()
            return carry
        lax.fori_loop(0, tm, body, 0, unroll=8)

    @pl.when(i == 0)
    def _():
        start_step(0)

    @pl.when(i + 1 < pl.num_programs(0))
    def _():
        start_step(i + 1)

    def wait(r, carry):
        row_copy(i, r, 0).wait()
        row_copy(i, r, 1).wait()
        return carry

    lax.fori_loop(0, tm, wait, 0, unroll=8)
    w = w_ref[...]
    o_ref[...] = x_ref[...] + w[:, 0:1] * ybuf[i % 2, 0] + w[:, 1:2] * ybuf[i % 2, 1]


def _combine(dest_flat, x, w_cols, ys, tm):
    T, D = x.shape
    tm = min(tm, T)
    return pl.pallas_call(
        functools.partial(_combine_kernel, n_tok=T),
        grid_spec=pltpu.PrefetchScalarGridSpec(
            num_scalar_prefetch=1,
            grid=(T // tm,),
            in_specs=[pl.BlockSpec((tm, D), lambda i, d: (i, 0)),
                      pl.BlockSpec((tm, 2), lambda i, d: (i, 0)),
                      pl.BlockSpec(memory_space=pl.ANY)],
            out_specs=pl.BlockSpec((tm, D), lambda i, d: (i, 0)),
            scratch_shapes=[pltpu.VMEM((2, 2, tm, D), F32), pltpu.SemaphoreType.DMA((2,))]),
        out_shape=jax.ShapeDtypeStruct((T, D), F32),
        compiler_params=_params(("arbitrary",)),
        name="combine",
    )(dest_flat, x, w_cols, ys)


def _ple_kernel(x_ref, g_ref, p_ref, wg_ref, wp_ref, o_ref, h_scr):
    tn = o_ref.shape[1]
    j = pl.program_id(1)

    @pl.when(j == 0)
    def _():
        _norm_rows_into(x_ref, g_ref, h_scr)

    gate = jax.nn.sigmoid(jnp.dot(h_scr[...], wg_ref[...], preferred_element_type=F32))
    proj = jnp.dot(p_ref[...].astype(BF16), wp_ref[...], preferred_element_type=F32)
    o_ref[...] = x_ref[:, pl.ds(pl.multiple_of(j * tn, tn), tn)] + gate * proj


def _ple(x, g, p, w_g, w_p, tm, tn):
    T, D = x.shape
    PD = p.shape[1]
    tm, tn = min(tm, T), min(tn, D)
    return pl.pallas_call(
        _ple_kernel,
        grid=(T // tm, D // tn),
        in_specs=[pl.BlockSpec((tm, D), lambda i, j: (i, 0)),
                  pl.BlockSpec((1, D), lambda i, j: (0, 0)),
                  pl.BlockSpec((tm, PD), lambda i, j: (i, 0)),
                  pl.BlockSpec((D, tn), lambda i, j: (0, j)),
                  pl.BlockSpec((PD, tn), lambda i, j: (0, j))],
        out_specs=pl.BlockSpec((tm, tn), lambda i, j: (i, j)),
        out_shape=jax.ShapeDtypeStruct((T, D), F32),
        scratch_shapes=[pltpu.VMEM((tm, D), BF16)],
        compiler_params=_params(("parallel", "arbitrary")),
        name="ple",
    )(x, g, p, w_g, w_p)


def _final_norm_kernel(x_ref, g_ref, o_ref):
    o_ref[...] = _rms(x_ref[...], g_ref[...])


def _final_norm(x, g, tm):
    T, D = x.shape
    tm = min(tm, T)
    return pl.pallas_call(
        _final_norm_kernel,
        grid=(T // tm,),
        in_specs=[pl.BlockSpec((tm, D), lambda i: (i, 0)), pl.BlockSpec((1, D), lambda i: (0, 0))],
        out_specs=pl.BlockSpec((tm, D), lambda i: (i, 0)),
        out_shape=jax.ShapeDtypeStruct((T, D), F32),
        compiler_params=_params(("parallel",)),
        name="final_norm",
    )(x, g)


def _rope_tables(S):
    pos = np.arange(S)
    inv_freq = ROPE_THETA ** (-np.arange(ROPE_PAIRS, dtype=np.float32) / ROPE_PAIRS)
    ang_row = jnp.asarray((pos // GRID_W).astype(np.float32)[:, None] * inv_freq[None, :])
    ang_col = jnp.asarray((pos % GRID_W).astype(np.float32)[:, None] * inv_freq[None, :])
    zero = jnp.zeros_like(ang_row)
    cos_t = jnp.concatenate([jnp.cos(ang_row)] * 2 + [jnp.cos(ang_col)] * 2, axis=1)
    sin_lo = jnp.concatenate([-jnp.sin(ang_row), zero, -jnp.sin(ang_col), zero], axis=1)
    sin_hi = jnp.concatenate([zero, jnp.sin(ang_row), zero, jnp.sin(ang_col)], axis=1)
    return cos_t, sin_lo, sin_hi


def kernel(x, p, norm1_g, w_in, mlstm_gate_b, mlstm_norm_g, q_norm_g, k_norm_g, w_branch_m, w_branch_a, w_out, norm2_g, w_router_group, b_router_group, w_router_expert, b_router_expert, w_exp_gate, w_exp_up, w_exp_down, ple_norm_g, w_ple_gate, w_ple_proj, final_norm_g):
    B, S, D = x.shape
    depth = w_in.shape[0]
    T = B * S
    H = MLSTM_HEADS
    nc = S // MLSTM_CHUNK
    scale = HEAD_DIM ** -0.5
    tables = _rope_tables(S)
    row = lambda v: v.reshape(1, -1).astype(F32)

    o_qm, o_gates = 0, 4 * MLSTM_W
    o_qa = o_gates + 4 * H
    o_gm = o_qa + ATTN_Q_W + 2 * ATTN_KV_W
    o_ga = o_gm + D
    z_qa = 4 * MLSTM_W
    z_gm = z_qa + ATTN_Q_W + 2 * ATTN_KV_W
    z_ga = z_gm + D

    xf = x.reshape(T, D)
    for i in range(depth):
        wi = w_in[i]
        w_main = jnp.concatenate([wi[:, o_qm:o_qm + MLSTM_W] * scale, wi[:, MLSTM_W:o_gates],
                                  wi[:, o_qa:]], axis=1).astype(BF16)
        w_gates = jnp.pad(wi[:, o_gates:o_qa], ((0, 0), (0, GATE_LANES - 4 * H))).astype(BF16)
        z, gates = _inproj(xf, row(norm1_g[i]), w_main, w_gates, 1024, 512)
        z3 = z.reshape(B, S, -1)

        gr = gates[:, :4 * H].reshape(B, nc, MLSTM_CHUNK, 4, H).transpose(0, 4, 1, 3, 2)
        gr = jnp.pad(gr, ((0, 0), (0, 0), (0, 0), (0, 4), (0, 0)))
        gb = jnp.pad(mlstm_gate_b[i].astype(F32).reshape(4, H).T, ((0, 0), (0, 4)))
        gb = jnp.broadcast_to(gb[:, :, None], (H, 8, MLSTM_CHUNK))
        y_m = _mlstm(z3, gr, gb, row(mlstm_norm_g[i]))
        y_a = _attention(z3, z_qa // HEAD_DIM, tables, row(q_norm_g[i]) * scale, row(k_norm_g[i]), 512)

        merged = _merge(y_m.reshape(T, MLSTM_W), y_a.reshape(T, ATTN_Q_W), z, z_gm, z_ga,
                        w_branch_m[i].astype(BF16), w_branch_a[i].astype(BF16), 1024, 512)
        xf = _outproj(merged, w_out[i].astype(BF16), xf, 1024, 512)

        wr = jnp.zeros((ROUTER_ROWS, D), F32)
        wr = wr.at[:N_GROUPS].set(w_router_group[i].T)
        wr = wr.at[ROUTER_EXPERT_ROW0:ROUTER_EXPERT_ROW0 + N_EXPERTS].set(w_router_expert[i].T)
        br = jnp.zeros((ROUTER_ROWS, 1), F32)
        br = br.at[:N_GROUPS, 0].set(b_router_group[i])
        br = br.at[ROUTER_EXPERT_ROW0:ROUTER_EXPERT_ROW0 + N_EXPERTS, 0].set(b_router_expert[i])
        idx, wts, cnt = _router(xf, row(norm2_g[i]), wr, br, 512)
        n_blk = (2 * T) // EXPERT_BLOCK + N_EXPERTS
        dest8, blk = _slots(idx, cnt, n_blk, 4096)
        dest = dest8[0:2].reshape(-1)
        slot_tok = _invert(dest, n_blk * EXPERT_BLOCK, T, 64)
        ys = _experts(blk[0, :n_blk], slot_tok, blk[1, 0:1], xf, row(norm2_g[i]),
                      w_exp_gate[i].astype(BF16), w_exp_up[i].astype(BF16),
                      w_exp_down[i].astype(BF16))
        xf = _combine(dest, xf, wts[0:2].T, ys, 256)

        xf = _ple(xf, row(ple_norm_g[i]), p[i].reshape(T, -1), w_ple_gate[i].astype(BF16),
                  w_ple_proj[i].astype(BF16), 1024, 512)

    return _final_norm(xf, row(final_norm_g), 512).reshape(B, S, D)
```

```python
import functools

import jax
import jax.numpy as jnp
import numpy as np
from jax import lax
from jax.experimental import pallas as pl
from jax.experimental.pallas import tpu as pltpu

F32 = jnp.float32
BF16 = jnp.bfloat16

GRID_W = 64
HEAD_DIM = 128
MLSTM_HEADS = 8
MLSTM_W = MLSTM_HEADS * HEAD_DIM
MLSTM_CHUNK = 128
MLSTM_HEADS_PER_STEP = 4
MLSTM_UNROLL = 4
MLSTM_DEN_ROWS = 16
ATTN_HEADS = 8
ATTN_KV_HEADS = 2
ATTN_GROUP = ATTN_HEADS // ATTN_KV_HEADS
ATTN_Q_W = ATTN_HEADS * HEAD_DIM
ATTN_KV_W = ATTN_KV_HEADS * HEAD_DIM
ROPE_THETA = 10000.0
ROPE_PAIRS = HEAD_DIM // 4
N_GROUPS = 4
EXPERTS_PER_GROUP = 8
N_EXPERTS = N_GROUPS * EXPERTS_PER_GROUP
MOE_BLOCK = 256
EPS = 1e-6
GATE_LANES = 128
ROUTER_ROWS = 128
ROUTER_EXPERT_ROW0 = 8
VMEM_LIMIT_BYTES = 52 * 1024 * 1024
NORM_ROWS = 128


def _params(semantics, **kw):
    return pltpu.CompilerParams(dimension_semantics=semantics,
                                vmem_limit_bytes=VMEM_LIMIT_BYTES, **kw)


def _rms(x, g):
    ms = jnp.mean(x * x, axis=-1, keepdims=True)
    return (x * lax.rsqrt(ms + EPS)) * g


def _norm_rows_into(x_ref, g_ref, h_scr):
    rows = x_ref.shape[0]
    step = min(NORM_ROWS, rows)

    def body(c, carry):
        r = pl.ds(pl.multiple_of(c * step, step), step)
        h_scr[r, :] = _rms(x_ref[r, :], g_ref[...]).astype(h_scr.dtype)
        return carry

    lax.fori_loop(0, rows // step, body, 0)


def _inproj_kernel(x_ref, g_ref, w_ref, wg_ref, z_ref, gates_ref, h_scr):
    @pl.when(pl.program_id(1) == 0)
    def _():
        _norm_rows_into(x_ref, g_ref, h_scr)
        gates_ref[...] = jnp.dot(h_scr[...], wg_ref[...], preferred_element_type=F32)

    z_ref[...] = jnp.dot(h_scr[...], w_ref[...], preferred_element_type=F32).astype(z_ref.dtype)


def _inproj(x, g, w, wg, tm, tn):
    T, D = x.shape
    N = w.shape[1]
    tm, tn = min(tm, T), min(tn, N)
    return pl.pallas_call(
        _inproj_kernel,
        grid=(T // tm, N // tn),
        in_specs=[pl.BlockSpec((tm, D), lambda i, j: (i, 0)),
                  pl.BlockSpec((1, D), lambda i, j: (0, 0)),
                  pl.BlockSpec((D, tn), lambda i, j: (0, j)),
                  pl.BlockSpec((D, 2 * GATE_LANES), lambda i, j: (0, 0))],
        out_specs=[pl.BlockSpec((tm, tn), lambda i, j: (i, j)),
                   pl.BlockSpec((tm, 2 * GATE_LANES), lambda i, j: (i, 0))],
        out_shape=[jax.ShapeDtypeStruct((T, N), BF16),
                   jax.ShapeDtypeStruct((T, 2 * GATE_LANES), F32)],
        scratch_shapes=[pltpu.VMEM((tm, D), BF16)],
        compiler_params=_params(("parallel", "arbitrary")),
        name="inproj",
    )(x, g, w, wg)


def _log_sigmoid(x):
    return jnp.minimum(x, 0.0) - jnp.log1p(jnp.exp(-jnp.abs(x)))


def _mlstm_kernel(q_ref, k_ref, v_ref, o_ref, gi_ref, gf_ref, bi_ref, bf_ref, ng_ref, y_ref,
                  up_scr, bt_scr, ut_scr, cm_scr, qt_scr, vxt_scr, a_scr, ht_scr, c_scr):
    L, d = MLSTM_CHUNK, HEAD_DIM
    S = q_ref.shape[1]
    nc = S // L
    heads = q_ref.shape[2] // d
    head0 = pl.program_id(1) * heads
    srow = lax.broadcasted_iota(jnp.int32, (L, L), 0)
    tcol = lax.broadcasted_iota(jnp.int32, (L, L), 1)
    mask2 = jnp.concatenate([srow <= tcol, srow >= tcol], axis=1)
    lasts = (L - 1, 0)
    tri2 = jnp.concatenate([jnp.where(tcol <= srow, 1.0, 0.0),
                            jnp.where(tcol >= srow, 1.0, 0.0)], axis=0).astype(BF16)
    fw_lane = lax.broadcasted_iota(jnp.int32, (L, GATE_LANES), 1) < MLSTM_HEADS
    ones_t = jnp.ones((MLSTM_DEN_ROWS, L), BF16)
    ch_row = lax.broadcasted_iota(jnp.int32, (GATE_LANES, 2 * L), 0)
    ch_fw = lax.broadcasted_iota(jnp.int32, (GATE_LANES, 2 * L), 1) < L

    def split3(x):
        p1 = x.astype(BF16)
        r1 = x - p1.astype(F32)
        p2 = r1.astype(BF16)
        p3 = (r1 - p2.astype(F32)).astype(BF16)
        return p1, p2, p3

    def chunk(c):
        return pl.ds(pl.multiple_of(c * L, L), L)

    def gates_pass(c, carry):
        sl = chunk(c)
        ls = _log_sigmoid(gf_ref[0, sl, :] + bf_ref[...])
        cs = sum(jnp.dot(tri2, part, preferred_element_type=F32) for part in split3(ls))
        bn = jnp.where(fw_lane, cs[:L], cs[L:])
        un = gi_ref[0, sl, :] + bi_ref[...] - bn
        for i, part in enumerate(split3(un)):
            up_scr[i, sl, :] = part
        bt_scr[c] = bn.T[:2 * MLSTM_HEADS]
        ut_scr[c] = un.T[:2 * MLSTM_HEADS]
        return carry

    def head_passes(hh):
        hl = slice(hh * d, (hh + 1) * d)
        head = head0 + hh
        sel2 = jnp.where(ch_row == jnp.where(ch_fw, head, MLSTM_HEADS + head), 1.0, 0.0).astype(BF16)

        def intra(c, carry):
            sl = chunk(c)
            q_t = q_ref[0, sl, hl].astype(F32).T.astype(BF16)
            vx_t = jnp.concatenate([v_ref[0, sl, hl].astype(F32).T.astype(BF16), ones_t], axis=0)
            qt_scr[c] = q_t
            vxt_scr[c] = vx_t
            kq = jnp.dot(k_ref[0, sl, hl], q_t, preferred_element_type=F32)
            ucol = sum(jnp.dot(up_scr[i, sl, :], sel2, preferred_element_type=F32)
                       for i in range(3))
            cmu = jnp.max(jnp.where(mask2, ucol, -jnp.inf), axis=0, keepdims=True)
            e0 = jnp.where(mask2, jnp.exp(ucol - cmu), 0.0)
            s0 = (jnp.concatenate([kq, kq], axis=1) * e0).astype(BF16)
            a_scr[c] = jnp.dot(vx_t, s0, preferred_element_type=F32)
            cm_scr[c] = cmu
            return carry

        def recurrence(j, ms):
            out = []
            for dd in range(2):
                c = j if dd == 0 else nc - 1 - j
                tl = slice(dd * L, (dd + 1) * L)
                m = ms[dd]
                row = pl.ds(dd * MLSTM_HEADS + head, 1)
                urow = ut_scr[c, row, :]
                brow = bt_scr[c, row, :]
                cmu = cm_scr[c, :, tl]
                b_last = brow[:, lasts[dd]:lasts[dd] + 1]
                u_max = cmu[:, lasts[dd]:lasts[dd] + 1]
                mm = jnp.maximum(cmu, m)
                cx = c_scr[dd]
                numx = (a_scr[c, :, tl] * jnp.exp(cmu - mm) + jnp.exp(m - mm)
                        * jnp.dot(cx.astype(BF16), qt_scr[c], preferred_element_type=F32))
                den = jnp.maximum(jnp.abs(numx[d:d + 1]), jnp.exp(-(brow + mm)))
                ht_scr[dd, c] = numx[:d] * (1.0 / den)
                m_new = b_last + jnp.maximum(m, u_max)
                wk = jnp.exp(b_last + urow - m_new)
                vw = (vxt_scr[c].astype(F32) * wk).astype(BF16)
                c_scr[dd] = (jnp.exp(b_last + m - m_new) * cx
                             + jnp.dot(vw, k_ref[0, chunk(c), hl], preferred_element_type=F32))
                out.append(m_new)
            return tuple(out)

        def finish(c, carry):
            sl = chunk(c)
            hn = _rms((ht_scr[0, c] + ht_scr[1, c]).T, ng_ref[:, hl])
            y_ref[0, sl, hl] = (hn * jax.nn.sigmoid(o_ref[0, sl, hl].astype(F32))).astype(y_ref.dtype)
            return carry

        c_scr[...] = jnp.zeros_like(c_scr)
        lax.fori_loop(0, nc, intra, 0, unroll=MLSTM_UNROLL)
        lax.fori_loop(0, nc, recurrence, (jnp.zeros((1, 1), F32),) * 2)
        lax.fori_loop(0, nc, finish, 0, unroll=MLSTM_UNROLL)

    lax.fori_loop(0, nc, gates_pass, 0, unroll=MLSTM_UNROLL)
    for hh in range(heads):
        head_passes(hh)


def _mlstm(z3, gates3, gate_bias, norm_g):
    B, S, _ = z3.shape
    H, hps = MLSTM_HEADS, MLSTM_HEADS_PER_STEP
    L, d = MLSTM_CHUNK, HEAD_DIM
    nc = S // L
    w = hps * d
    rows = d + MLSTM_DEN_ROWS
    col = lambda part: pl.BlockSpec((1, S, w), lambda b, h: (b, 0, part * (H // hps) + h))
    gate = lambda part: pl.BlockSpec((1, S, GATE_LANES), lambda b, h: (b, 0, part))
    bias = lambda part: pl.BlockSpec((1, GATE_LANES), lambda b, h: (0, part))
    return pl.pallas_call(
        _mlstm_kernel,
        grid=(B, H // hps),
        in_specs=[col(0), col(1), col(2), col(3), gate(0), gate(1), bias(0), bias(1),
                  pl.BlockSpec((1, w), lambda b, h: (0, h))],
        out_specs=pl.BlockSpec((1, S, w), lambda b, h: (b, 0, h)),
        out_shape=jax.ShapeDtypeStruct((B, S, MLSTM_W), BF16),
        scratch_shapes=[pltpu.VMEM((3, S, GATE_LANES), BF16),
                        pltpu.VMEM((nc, 2 * H, L), F32),
                        pltpu.VMEM((nc, 2 * H, L), F32),
                        pltpu.VMEM((nc, 1, 2 * L), F32),
                        pltpu.VMEM((nc, d, L), BF16),
                        pltpu.VMEM((nc, rows, L), BF16),
                        pltpu.VMEM((nc, rows, 2 * L), F32),
                        pltpu.VMEM((2, nc, d, L), F32),
                        pltpu.VMEM((2, rows, d), F32)],
        compiler_params=_params(("parallel", "arbitrary")),
        name="mlstm",
    )(z3, z3, z3, z3, gates3, gates3, gate_bias, gate_bias, norm_g)


def _rope(t, c, s_lo, s_hi):
    quarter = HEAD_DIM // 4
    return (t * c + pltpu.roll(t, HEAD_DIM - quarter, 1) * s_lo
            + pltpu.roll(t, quarter, 1) * s_hi)


def _attn_kernel(q_ref, k_ref, v_ref, cq_ref, sloq_ref, shiq_ref, ck_ref, slok_ref, shik_ref,
                 qg_ref, kg_ref, o_ref, k_scr):
    S = k_ref.shape[1]
    step = min(256, S)

    @pl.when(pl.program_id(2) == 0)
    def _():
        def body(c, carry):
            r = pl.ds(pl.multiple_of(c * step, step), step)
            kn = _rms(k_ref[0, r, :].astype(F32), kg_ref[...])
            k_scr[r, :] = _rope(kn, ck_ref[r, :], slok_ref[r, :], shik_ref[r, :]).astype(BF16)
            return carry

        lax.fori_loop(0, S // step, body, 0)

    for g in range(ATTN_GROUP):
        hl = slice(g * HEAD_DIM, (g + 1) * HEAD_DIM)
        qn = _rms(q_ref[0, :, hl].astype(F32), qg_ref[...])
        qr = _rope(qn, cq_ref[...], sloq_ref[...], shiq_ref[...]).astype(BF16)
        s = lax.dot_general(qr, k_scr[...], (((1,), (1,)), ((), ())), preferred_element_type=F32)
        p = jnp.exp2(s - jnp.max(s, axis=1, keepdims=True))
        l = jnp.sum(p, axis=1, keepdims=True)
        o = jnp.dot(p.astype(BF16), v_ref[0], preferred_element_type=F32)
        o_ref[0, :, hl] = (o / l).astype(o_ref.dtype)


def _attention(z3, q_blk0, tables, q_g, k_g, tq):
    B, S, _ = z3.shape
    tq = min(tq, S)
    cos_t, slo_t, shi_t = tables
    gw = ATTN_GROUP * HEAD_DIM
    q_grp0 = q_blk0 // ATTN_GROUP
    assert q_grp0 * ATTN_GROUP == q_blk0
    k_blk0 = q_blk0 + ATTN_HEADS
    v_blk0 = k_blk0 + ATTN_KV_HEADS
    qtab = pl.BlockSpec((tq, HEAD_DIM), lambda b, kv, i: (i, 0))
    ktab = pl.BlockSpec((S, HEAD_DIM), lambda b, kv, i: (0, 0))
    gain = pl.BlockSpec((1, HEAD_DIM), lambda b, kv, i: (0, 0))
    return pl.pallas_call(
        _attn_kernel,
        grid=(B, ATTN_KV_HEADS, S // tq),
        in_specs=[pl.BlockSpec((1, tq, gw), lambda b, kv, i: (b, i, q_grp0 + kv)),
                  pl.BlockSpec((1, S, HEAD_DIM), lambda b, kv, i: (b, 0, k_blk0 + kv)),
                  pl.BlockSpec((1, S, HEAD_DIM), lambda b, kv, i: (b, 0, v_blk0 + kv)),
                  qtab, qtab, qtab, ktab, ktab, ktab, gain, gain],
        out_specs=pl.BlockSpec((1, tq, gw), lambda b, kv, i: (b, i, kv)),
        out_shape=jax.ShapeDtypeStruct((B, S, ATTN_Q_W), BF16),
        scratch_shapes=[pltpu.VMEM((S, HEAD_DIM), BF16)],
        compiler_params=_params(("parallel", "parallel", "arbitrary")),
        name="attention",
    )(z3, z3, z3, cos_t, slo_t, shi_t, cos_t, slo_t, shi_t, q_g, k_g)


def _merge_kernel(ym_ref, ya_ref, gm_ref, ga_ref, wm_ref, wa_ref, o_ref):
    bm = jnp.dot(ym_ref[...], wm_ref[...], preferred_element_type=F32)
    ba = jnp.dot(ya_ref[...], wa_ref[...], preferred_element_type=F32)
    o_ref[...] = (jax.nn.sigmoid(gm_ref[...].astype(F32)) * bm
                  + jax.nn.sigmoid(ga_ref[...].astype(F32)) * ba).astype(o_ref.dtype)


def _merge(y_m, y_a, z, gm_col0, ga_col0, w_m, w_a, tm, tn):
    T, K = y_m.shape
    D = w_m.shape[1]
    tm, tn = min(tm, T), min(tn, D)
    gm0, ga0 = gm_col0 // tn, ga_col0 // tn
    assert gm0 * tn == gm_col0 and ga0 * tn == ga_col0
    return pl.pallas_call(
        _merge_kernel,
        grid=(T // tm, D // tn),
        in_specs=[pl.BlockSpec((tm, K), lambda i, j: (i, 0)),
                  pl.BlockSpec((tm, K), lambda i, j: (i, 0)),
                  pl.BlockSpec((tm, tn), lambda i, j: (i, gm0 + j)),
                  pl.BlockSpec((tm, tn), lambda i, j: (i, ga0 + j)),
                  pl.BlockSpec((K, tn), lambda i, j: (0, j)),
                  pl.BlockSpec((K, tn), lambda i, j: (0, j))],
        out_specs=pl.BlockSpec((tm, tn), lambda i, j: (i, j)),
        out_shape=jax.ShapeDtypeStruct((T, D), BF16),
        compiler_params=_params(("parallel", "arbitrary")),
        name="merge",
    )(y_m, y_a, z, z, w_m, w_a)


def _outproj_kernel(a_ref, w_ref, x_ref, o_ref):
    o_ref[...] = x_ref[...] + jnp.dot(a_ref[...], w_ref[...], preferred_element_type=F32)


def _outproj(a, w, x, tm, tn):
    T, K = a.shape
    D = w.shape[1]
    tm, tn = min(tm, T), min(tn, D)
    return pl.pallas_call(
        _outproj_kernel,
        grid=(T // tm, D // tn),
        in_specs=[pl.BlockSpec((tm, K), lambda i, j: (i, 0)),
                  pl.BlockSpec((K, tn), lambda i, j: (0, j)),
                  pl.BlockSpec((tm, tn), lambda i, j: (i, j))],
        out_specs=pl.BlockSpec((tm, tn), lambda i, j: (i, j)),
        out_shape=jax.ShapeDtypeStruct((T, D), F32),
        compiler_params=_params(("parallel", "arbitrary")),
        name="outproj",
    )(a, w, x)


def _first_argmax(vals, vmax):
    idx = jnp.full(vals[0].shape, len(vals), jnp.int32)
    for j in reversed(range(len(vals))):
        idx = jnp.where(vals[j] == vmax, j, idx)
    return idx


def _router_kernel(x_ref, g_ref, wr_ref, br_ref, idx_ref, wts_ref, cnt_ref, tri_scr, carry_scr):
    tm = x_ref.shape[0]

    @pl.when(pl.program_id(0) == 0)
    def _():
        r = lax.broadcasted_iota(jnp.int32, (tm, tm), 0)
        c = lax.broadcasted_iota(jnp.int32, (tm, tm), 1)
        tri_scr[...] = jnp.where(r < c, 1.0, 0.0).astype(BF16)
        carry_scr[...] = jnp.zeros_like(carry_scr)

    h = _rms(x_ref[...], g_ref[...])
    logits = lax.dot_general(wr_ref[...], h, (((1,), (1,)), ((), ())),
                             preferred_element_type=F32,
                             precision=lax.Precision.HIGHEST) + br_ref[...]
    gl = [logits[j:j + 1, :] for j in range(N_GROUPS)]
    gmax = functools.reduce(jnp.maximum, gl)
    gi = _first_argmax(gl, gmax)
    gp = 1.0 / functools.reduce(jnp.add, [jnp.exp(v - gmax) for v in gl])
    eg = []
    for e in range(EXPERTS_PER_GROUP):
        v = logits[ROUTER_EXPERT_ROW0 + e:ROUTER_EXPERT_ROW0 + e + 1, :]
        for grp in range(1, N_GROUPS):
            r0 = ROUTER_EXPERT_ROW0 + grp * EXPERTS_PER_GROUP + e
            v = jnp.where(gi == grp, logits[r0:r0 + 1, :], v)
        eg.append(v)
    v1 = functools.reduce(jnp.maximum, eg)
    i1 = _first_argmax(eg, v1)
    eg2 = [jnp.where(i1 == e, -jnp.inf, eg[e]) for e in range(EXPERTS_PER_GROUP)]
    v2 = functools.reduce(jnp.maximum, eg2)
    i2 = _first_argmax(eg2, v2)
    e2 = jnp.exp(v2 - v1)
    w1 = gp / (1.0 + e2)
    w2 = gp * e2 / (1.0 + e2)
    eid1 = gi * EXPERTS_PER_GROUP + i1
    eid2 = gi * EXPERTS_PER_GROUP + i2

    erow = lax.broadcasted_iota(jnp.int32, (N_EXPERTS, tm), 0)
    oh1 = erow == eid1
    oh2 = erow == eid2
    oh = jnp.where(oh1 | oh2, 1.0, 0.0)
    before = jnp.dot(oh.astype(BF16), tri_scr[...], preferred_element_type=F32) + carry_scr[...]
    rank1 = jnp.sum(jnp.where(oh1, before, 0.0), axis=0, keepdims=True)
    rank2 = jnp.sum(jnp.where(oh2, before, 0.0), axis=0, keepdims=True)
    carry_scr[...] = carry_scr[...] + jnp.sum(oh, axis=1, keepdims=True)

    idx_ref[...] = jnp.zeros_like(idx_ref)
    idx_ref[0:1, :] = eid1
    idx_ref[1:2, :] = eid2
    idx_ref[2:3, :] = rank1.astype(jnp.int32)
    idx_ref[3:4, :] = rank2.astype(jnp.int32)
    wts_ref[...] = jnp.zeros_like(wts_ref)
    wts_ref[0:1, :] = w1
    wts_ref[1:2, :] = w2
    cnt_ref[...] = jnp.broadcast_to(carry_scr[...], cnt_ref.shape).astype(jnp.int32)


def _router(x, g, wr, br, tm):
    T, D = x.shape
    tm = min(tm, T)
    return pl.pallas_call(
        _router_kernel,
        grid=(T // tm,),
        in_specs=[pl.BlockSpec((tm, D), lambda i: (i, 0)),
                  pl.BlockSpec((1, D), lambda i: (0, 0)),
                  pl.BlockSpec((ROUTER_ROWS, D), lambda i: (0, 0)),
                  pl.BlockSpec((ROUTER_ROWS, 1), lambda i: (0, 0))],
        out_specs=[pl.BlockSpec((8, tm), lambda i: (0, i)),
                   pl.BlockSpec((8, tm), lambda i: (0, i)),
                   pl.BlockSpec((N_EXPERTS, 128), lambda i: (0, 0))],
        out_shape=[jax.ShapeDtypeStruct((8, T), jnp.int32),
                   jax.ShapeDtypeStruct((8, T), F32),
                   jax.ShapeDtypeStruct((N_EXPERTS, 128), jnp.int32)],
        scratch_shapes=[pltpu.VMEM((tm, tm), BF16), pltpu.VMEM((N_EXPERTS, 1), F32)],
        compiler_params=_params(("arbitrary",)),
        name="router",
    )(x, g, wr, br)


def _slots_kernel(idx_ref, cnt_ref, dest_ref, blk_ref):
    shift = MOE_BLOCK.bit_length() - 1
    cnt = cnt_ref[...]
    padded = ((cnt + (MOE_BLOCK - 1)) >> shift) << shift
    eid1, eid2 = idx_ref[0:1, :], idx_ref[1:2, :]
    d1, d2 = idx_ref[2:3, :], idx_ref[3:4, :]
    blk_start = lax.broadcasted_iota(jnp.int32, (1, blk_ref.shape[1]), 1) * MOE_BLOCK
    blk_e = jnp.zeros_like(blk_start)
    acc = jnp.zeros((1, 1), jnp.int32)
    for e in range(N_EXPERTS):
        d1 = d1 + jnp.where(eid1 == e, acc, 0)
        d2 = d2 + jnp.where(eid2 == e, acc, 0)
        acc = acc + padded[e:e + 1, 0:1]
        blk_e = blk_e + jnp.where(acc <= blk_start, 1, 0)
    dest_ref[...] = jnp.zeros_like(dest_ref)
    dest_ref[0:1, :] = d1
    dest_ref[1:2, :] = d2
    blk_ref[...] = jnp.zeros_like(blk_ref)
    blk_ref[0:1, :] = jnp.minimum(blk_e, N_EXPERTS - 1)


def _slots(idx, cnt, n_blk, tm):
    T = idx.shape[1]
    tm = min(tm, T)
    lanes = -(-n_blk // 128) * 128
    return pl.pallas_call(
        _slots_kernel,
        grid=(T // tm,),
        in_specs=[pl.BlockSpec((8, tm), lambda i: (0, i)),
                  pl.BlockSpec((N_EXPERTS, 128), lambda i: (0, 0))],
        out_specs=[pl.BlockSpec((8, tm), lambda i: (0, i)),
                   pl.BlockSpec((8, lanes), lambda i: (0, 0))],
        out_shape=[jax.ShapeDtypeStruct((8, T), jnp.int32),
                   jax.ShapeDtypeStruct((8, lanes), jnp.int32)],
        compiler_params=_params(("arbitrary",)),
        name="slots",
    )(idx, cnt)


def _invert_kernel(dest_ref, slot_ref, *, n_tok, slot_chunk, tok_chunk):
    phase, step = pl.program_id(0), pl.program_id(1)

    @pl.when(phase == 0)
    def _():
        def zero(s, carry):
            slot_ref[step * slot_chunk + s] = 0
            return carry
        lax.fori_loop(0, slot_chunk, zero, 0, unroll=8)

    @pl.when(phase == 1)
    def _():
        def put(r, carry):
            t = step * tok_chunk + r
            slot_ref[dest_ref[t]] = t
            slot_ref[dest_ref[n_tok + t]] = t
            return carry
        lax.fori_loop(0, tok_chunk, put, 0, unroll=8)


def _invert(dest_flat, n_slots, n_tok, steps):
    while n_slots % steps or n_tok % steps:
        steps //= 2
    return pl.pallas_call(
        functools.partial(_invert_kernel, n_tok=n_tok, slot_chunk=n_slots // steps,
                          tok_chunk=n_tok // steps),
        grid=(2, steps),
        in_specs=[pl.BlockSpec(memory_space=pltpu.SMEM)],
        out_specs=pl.BlockSpec(memory_space=pltpu.SMEM),
        out_shape=jax.ShapeDtypeStruct((n_slots,), jnp.int32),
        compiler_params=_params(("arbitrary", "arbitrary")),
        name="invert",
    )(dest_flat)


def _expert_kernel(be_ref, tok_ref, x_hbm, g_ref, wg_ref, wu_ref, wd_ref, ys_ref, xbuf, sem):
    del be_ref
    i = pl.program_id(0)
    last = pl.num_programs(0) - 1

    def row_copy(blk, buf, r):
        t = tok_ref[blk * MOE_BLOCK + r]
        return pltpu.make_async_copy(x_hbm.at[pl.ds(t, 1)], xbuf.at[buf, pl.ds(r, 1)], sem.at[buf])

    def all_rows(fn):
        def body(r, carry):
            fn(r)
            return carry
        lax.fori_loop(0, MOE_BLOCK, body, 0, unroll=8)

    @pl.when(i == 0)
    def _():
        all_rows(lambda r: row_copy(0, 0, r).start())

    all_rows(lambda r: row_copy(i, i % 2, r).wait())

    nxt = jnp.minimum(i + 1, last)
    for r in range(MOE_BLOCK):
        row_copy(nxt, (i + 1) % 2, r).start()
    h = _rms(xbuf[i % 2], g_ref[...]).astype(BF16)
    a = jnp.dot(h, wg_ref[0], preferred_element_type=F32)
    u = jnp.dot(h, wu_ref[0], preferred_element_type=F32)
    act = (a * jax.nn.sigmoid(a) * u).astype(BF16)
    ys_ref[...] = jnp.dot(act, wd_ref[0], preferred_element_type=F32)

    @pl.when(i == last)
    def _():
        all_rows(lambda r: row_copy(nxt, (i + 1) % 2, r).wait())


def _experts(blk_e, slot_tok, x, g, w_g, w_u, w_d):
    D = x.shape[1]
    FF = w_g.shape[2]
    n_blk = blk_e.shape[0]
    return pl.pallas_call(
        _expert_kernel,
        grid_spec=pltpu.PrefetchScalarGridSpec(
            num_scalar_prefetch=2,
            grid=(n_blk,),
            in_specs=[pl.BlockSpec(memory_space=pl.ANY),
                      pl.BlockSpec((1, D), lambda i, be, tok: (0, 0)),
                      pl.BlockSpec((1, D, FF), lambda i, be, tok: (be[i], 0, 0)),
                      pl.BlockSpec((1, D, FF), lambda i, be, tok: (be[i], 0, 0)),
                      pl.BlockSpec((1, FF, D), lambda i, be, tok: (be[i], 0, 0))],
            out_specs=pl.BlockSpec((MOE_BLOCK, D), lambda i, be, tok: (i, 0)),
            scratch_shapes=[pltpu.VMEM((2, MOE_BLOCK, D), F32),
                            pltpu.SemaphoreType.DMA((2,))]),
        out_shape=jax.ShapeDtypeStruct((n_blk * MOE_BLOCK, D), F32),
        compiler_params=_params(("arbitrary",)),
        name="experts",
    )(blk_e, slot_tok, x, g, w_g, w_u, w_d)


def _combine_kernel(dest_ref, x_ref, w_ref, ys_hbm, o_ref, ybuf, sem, *, n_tok):
    tm = x_ref.shape[0]
    i = pl.program_id(0)

    def row_copy(step, r, k):
        d = dest_ref[k * n_tok + step * tm + r]
        buf = step % 2
        return pltpu.make_async_copy(ys_hbm.at[pl.ds(d, 1)], ybuf.at[buf, k, pl.ds(r, 1)],
                                     sem.at[buf])

    def start_step(step):
        def body(r, carry):
            row_copy(step, r, 0).start()
            row_copy(step, r, 1).start()
            return carry
        lax.fori_loop(0, tm, body, 0, unroll=8)

    @pl.when(i == 0)
    def _():
        start_step(0)

    @pl.when(i + 1 < pl.num_programs(0))
    def _():
        start_step(i + 1)

    def wait(r, carry):
        row_copy(i, r, 0).wait()
        row_copy(i, r, 1).wait()
        return carry

    lax.fori_loop(0, tm, wait, 0, unroll=8)
    w = w_ref[...]
    o_ref[...] = x_ref[...] + w[:, 0:1] * ybuf[i % 2, 0] + w[:, 1:2] * ybuf[i % 2, 1]


def _combine(dest_flat, x, w_cols, ys, tm):
    T, D = x.shape
    tm = min(tm, T)
    return pl.pallas_call(
        functools.partial(_combine_kernel, n_tok=T),
        grid_spec=pltpu.PrefetchScalarGridSpec(
            num_scalar_prefetch=1,
            grid=(T // tm,),
            in_specs=[pl.BlockSpec((tm, D), lambda i, d: (i, 0)),
                      pl.BlockSpec((tm, 2), lambda i, d: (i, 0)),
                      pl.BlockSpec(memory_space=pl.ANY)],
            out_specs=pl.BlockSpec((tm, D), lambda i, d: (i, 0)),
            scratch_shapes=[pltpu.VMEM((2, 2, tm, D), F32), pltpu.SemaphoreType.DMA((2,))]),
        out_shape=jax.ShapeDtypeStruct((T, D), F32),
        compiler_params=_params(("arbitrary",)),
        name="combine",
    )(dest_flat, x, w_cols, ys)


def _ple_kernel(x_ref, g_ref, p_ref, wg_ref, wp_ref, o_ref, h_scr):
    tn = o_ref.shape[1]
    j = pl.program_id(1)

    @pl.when(j == 0)
    def _():
        _norm_rows_into(x_ref, g_ref, h_scr)

    gate = jax.nn.sigmoid(jnp.dot(h_scr[...], wg_ref[...], preferred_element_type=F32))
    proj = jnp.dot(p_ref[...].astype(BF16), wp_ref[...], preferred_element_type=F32)
    o_ref[...] = x_ref[:, pl.ds(pl.multiple_of(j * tn, tn), tn)] + gate * proj


def _ple(x, g, p, w_g, w_p, tm, tn):
    T, D = x.shape
    PD = p.shape[1]
    tm, tn = min(tm, T), min(tn, D)
    return pl.pallas_call(
        _ple_kernel,
        grid=(T // tm, D // tn),
        in_specs=[pl.BlockSpec((tm, D), lambda i, j: (i, 0)),
                  pl.BlockSpec((1, D), lambda i, j: (0, 0)),
                  pl.BlockSpec((tm, PD), lambda i, j: (i, 0)),
                  pl.BlockSpec((D, tn), lambda i, j: (0, j)),
                  pl.BlockSpec((PD, tn), lambda i, j: (0, j))],
        out_specs=pl.BlockSpec((tm, tn), lambda i, j: (i, j)),
        out_shape=jax.ShapeDtypeStruct((T, D), F32),
        scratch_shapes=[pltpu.VMEM((tm, D), BF16)],
        compiler_params=_params(("parallel", "arbitrary")),
        name="ple",
    )(x, g, p, w_g, w_p)


def _final_norm_kernel(x_ref, g_ref, o_ref):
    o_ref[...] = _rms(x_ref[...], g_ref[...])


def _final_norm(x, g, tm):
    T, D = x.shape
    tm = min(tm, T)
    return pl.pallas_call(
        _final_norm_kernel,
        grid=(T // tm,),
        in_specs=[pl.BlockSpec((tm, D), lambda i: (i, 0)), pl.BlockSpec((1, D), lambda i: (0, 0))],
        out_specs=pl.BlockSpec((tm, D), lambda i: (i, 0)),
        out_shape=jax.ShapeDtypeStruct((T, D), F32),
        compiler_params=_params(("parallel",)),
        name="final_norm",
    )(x, g)


def _rope_tables(S):
    pos = np.arange(S)
    inv_freq = ROPE_THETA ** (-np.arange(ROPE_PAIRS, dtype=np.float32) / ROPE_PAIRS)
    ang_row = jnp.asarray((pos // GRID_W).astype(np.float32)[:, None] * inv_freq[None, :])
    ang_col = jnp.asarray((pos % GRID_W).astype(np.float32)[:, None] * inv_freq[None, :])
    zero = jnp.zeros_like(ang_row)
    cos_t = jnp.concatenate([jnp.cos(ang_row)] * 2 + [jnp.cos(ang_col)] * 2, axis=1)
    sin_lo = jnp.concatenate([-jnp.sin(ang_row), zero, -jnp.sin(ang_col), zero], axis=1)
    sin_hi = jnp.concatenate([zero, jnp.sin(ang_row), zero, jnp.sin(ang_col)], axis=1)
    return cos_t, sin_lo, sin_hi


def kernel(x, p, norm1_g, w_in, mlstm_gate_b, mlstm_norm_g, q_norm_g, k_norm_g, w_branch_m, w_branch_a, w_out, norm2_g, w_router_group, b_router_group, w_router_expert, b_router_expert, w_exp_gate, w_exp_up, w_exp_down, ple_norm_g, w_ple_gate, w_ple_proj, final_norm_g):
    B, S, D = x.shape
    depth = w_in.shape[0]
    T = B * S
    H = MLSTM_HEADS
    scale = HEAD_DIM ** -0.5
    tables = _rope_tables(S)
    row = lambda v: v.reshape(1, -1).astype(F32)

    o_qm, o_gates = 0, 4 * MLSTM_W
    o_qa = o_gates + 4 * H
    o_gm = o_qa + ATTN_Q_W + 2 * ATTN_KV_W
    o_ga = o_gm + D
    z_qa = 4 * MLSTM_W
    z_gm = z_qa + ATTN_Q_W + 2 * ATTN_KV_W
    z_ga = z_gm + D

    xf = x.reshape(T, D)
    for i in range(depth):
        wi = w_in[i]
        w_main = jnp.concatenate([wi[:, o_qm:o_qm + MLSTM_W] * scale, wi[:, MLSTM_W:o_gates],
                                  wi[:, o_qa:]], axis=1).astype(BF16)
        lane_pad = ((0, 0), (0, GATE_LANES - 2 * H))
        w_gates = jnp.concatenate([jnp.pad(wi[:, o_gates:o_gates + 2 * H], lane_pad),
                                   jnp.pad(wi[:, o_gates + 2 * H:o_qa], lane_pad)], axis=1).astype(BF16)
        gb = mlstm_gate_b[i].astype(F32).reshape(1, 4 * H)
        gate_bias = jnp.concatenate([jnp.pad(gb[:, :2 * H], lane_pad),
                                     jnp.pad(gb[:, 2 * H:], lane_pad)], axis=1)
        z, gates = _inproj(xf, row(norm1_g[i]), w_main, w_gates, 1024, 512)
        z3 = z.reshape(B, S, -1)
        y_m = _mlstm(z3, gates.reshape(B, S, -1), gate_bias, row(mlstm_norm_g[i]))
        y_a = _attention(z3, z_qa // HEAD_DIM, tables, row(q_norm_g[i]) * (scale * np.log2(np.e)),
                         row(k_norm_g[i]), 256)

        merged = _merge(y_m.reshape(T, MLSTM_W), y_a.reshape(T, ATTN_Q_W), z, z_gm, z_ga,
                        w_branch_m[i].astype(BF16), w_branch_a[i].astype(BF16), 1024, 512)
        xf = _outproj(merged, w_out[i].astype(BF16), xf, 1024, 512)

        wr = jnp.zeros((ROUTER_ROWS, D), F32)
        wr = wr.at[:N_GROUPS].set(w_router_group[i].T)
        wr = wr.at[ROUTER_EXPERT_ROW0:ROUTER_EXPERT_ROW0 + N_EXPERTS].set(w_router_expert[i].T)
        br = jnp.zeros((ROUTER_ROWS, 1), F32)
        br = br.at[:N_GROUPS, 0].set(b_router_group[i])
        br = br.at[ROUTER_EXPERT_ROW0:ROUTER_EXPERT_ROW0 + N_EXPERTS, 0].set(b_router_expert[i])
        idx, wts, cnt = _router(xf, row(norm2_g[i]), wr, br, 512)
        n_blk = -(-2 * T // MOE_BLOCK) + N_EXPERTS
        dest8, blk = _slots(idx, cnt, n_blk, 4096)
        dest = dest8[0:2].reshape(-1)
        slot_tok = _invert(dest, n_blk * MOE_BLOCK, T, 64)
        ys = _experts(blk[0, :n_blk], slot_tok, xf, row(norm2_g[i]),
                      w_exp_gate[i].astype(BF16), w_exp_up[i].astype(BF16),
                      w_exp_down[i].astype(BF16))
        xf = _combine(dest, xf, wts[0:2].T, ys, 256)

        xf = _ple(xf, row(ple_norm_g[i]), p[i].reshape(T, -1), w_ple_gate[i].astype(BF16),
                  w_ple_proj[i].astype(BF16), 1024, 512)

    return _final_norm(xf, row(final_norm_g), 512).reshape(B, S, D)
```

```python
import functools

import jax
import jax.numpy as jnp
import numpy as np
from jax import lax
from jax.experimental import pallas as pl
from jax.experimental.pallas import tpu as pltpu

F32 = jnp.float32
BF16 = jnp.bfloat16

GRID_W = 64
HEAD_DIM = 128
MLSTM_HEADS = 8
MLSTM_W = MLSTM_HEADS * HEAD_DIM
MLSTM_CHUNK = 128
MLSTM_HEADS_PER_STEP = 4
MLSTM_UNROLL = 4
MLSTM_DEN_ROWS = 16
ATTN_HEADS = 8
ATTN_KV_HEADS = 2
ATTN_GROUP = ATTN_HEADS // ATTN_KV_HEADS
ATTN_Q_W = ATTN_HEADS * HEAD_DIM
ATTN_KV_W = ATTN_KV_HEADS * HEAD_DIM
ROPE_THETA = 10000.0
ROPE_PAIRS = HEAD_DIM // 4
N_GROUPS = 4
EXPERTS_PER_GROUP = 8
N_EXPERTS = N_GROUPS * EXPERTS_PER_GROUP
MOE_BLOCK = 256
EPS = 1e-6
GATE_LANES = 128
ROUTER_ROWS = 128
ROUTER_EXPERT_ROW0 = 8
VMEM_LIMIT_BYTES = 52 * 1024 * 1024
NORM_ROWS = 128


def _params(semantics, **kw):
    return pltpu.CompilerParams(dimension_semantics=semantics,
                                vmem_limit_bytes=VMEM_LIMIT_BYTES, **kw)


def _rms(x, g):
    ms = jnp.mean(x * x, axis=-1, keepdims=True)
    return (x * lax.rsqrt(ms + EPS)) * g


def _norm_rows_into(x_ref, g_ref, h_scr):
    rows = x_ref.shape[0]
    step = min(NORM_ROWS, rows)

    def body(c, carry):
        r = pl.ds(pl.multiple_of(c * step, step), step)
        h_scr[r, :] = _rms(x_ref[r, :], g_ref[...]).astype(h_scr.dtype)
        return carry

    lax.fori_loop(0, rows // step, body, 0)


def _inproj_kernel(x_ref, g_ref, w_ref, wg_ref, z_ref, gates_ref, h_scr):
    @pl.when(pl.program_id(1) == 0)
    def _():
        _norm_rows_into(x_ref, g_ref, h_scr)
        gates_ref[...] = jnp.dot(h_scr[...], wg_ref[...], preferred_element_type=F32)

    z_ref[...] = jnp.dot(h_scr[...], w_ref[...], preferred_element_type=F32).astype(z_ref.dtype)


def _inproj(x, g, w, wg, tm, tn):
    T, D = x.shape
    N = w.shape[1]
    tm, tn = min(tm, T), min(tn, N)
    return pl.pallas_call(
        _inproj_kernel,
        grid=(T // tm, N // tn),
        in_specs=[pl.BlockSpec((tm, D), lambda i, j: (i, 0)),
                  pl.BlockSpec((1, D), lambda i, j: (0, 0)),
                  pl.BlockSpec((D, tn), lambda i, j: (0, j)),
                  pl.BlockSpec((D, 2 * GATE_LANES), lambda i, j: (0, 0))],
        out_specs=[pl.BlockSpec((tm, tn), lambda i, j: (i, j)),
                   pl.BlockSpec((tm, 2 * GATE_LANES), lambda i, j: (i, 0))],
        out_shape=[jax.ShapeDtypeStruct((T, N), BF16),
                   jax.ShapeDtypeStruct((T, 2 * GATE_LANES), F32)],
        scratch_shapes=[pltpu.VMEM((tm, D), BF16)],
        compiler_params=_params(("parallel", "arbitrary")),
        name="inproj",
    )(x, g, w, wg)


def _log_sigmoid(x):
    return jnp.minimum(x, 0.0) - jnp.log1p(jnp.exp(-jnp.abs(x)))


def _mlstm_kernel(q_ref, k_ref, v_ref, o_ref, gi_ref, gf_ref, bi_ref, bf_ref, ng_ref, y_ref,
                  up_scr, bt_scr, ut_scr, cm_scr, qt_scr, vxt_scr, a_scr, ht_scr, c_scr):
    L, d = MLSTM_CHUNK, HEAD_DIM
    S = q_ref.shape[1]
    nc = S // L
    heads = q_ref.shape[2] // d
    head0 = pl.program_id(1) * heads
    srow = lax.broadcasted_iota(jnp.int32, (L, L), 0)
    tcol = lax.broadcasted_iota(jnp.int32, (L, L), 1)
    mask2 = jnp.concatenate([srow <= tcol, srow >= tcol], axis=1)
    lasts = (L - 1, 0)
    tri2 = jnp.concatenate([jnp.where(tcol <= srow, 1.0, 0.0),
                            jnp.where(tcol >= srow, 1.0, 0.0)], axis=0).astype(BF16)
    fw_lane = lax.broadcasted_iota(jnp.int32, (L, GATE_LANES), 1) < MLSTM_HEADS
    ones_t = jnp.ones((MLSTM_DEN_ROWS, L), BF16)
    ch_row = lax.broadcasted_iota(jnp.int32, (GATE_LANES, 2 * L), 0)
    ch_fw = lax.broadcasted_iota(jnp.int32, (GATE_LANES, 2 * L), 1) < L

    def split3(x):
        p1 = x.astype(BF16)
        r1 = x - p1.astype(F32)
        p2 = r1.astype(BF16)
        p3 = (r1 - p2.astype(F32)).astype(BF16)
        return p1, p2, p3

    def chunk(c):
        return pl.ds(pl.multiple_of(c * L, L), L)

    def gates_pass(c, carry):
        sl = chunk(c)
        ls = _log_sigmoid(gf_ref[0, sl, :] + bf_ref[...])
        cs = sum(jnp.dot(tri2, part, preferred_element_type=F32) for part in split3(ls))
        bn = jnp.where(fw_lane, cs[:L], cs[L:])
        un = gi_ref[0, sl, :] + bi_ref[...] - bn
        for i, part in enumerate(split3(un)):
            up_scr[i, sl, :] = part
        bt_scr[c] = bn.T[:2 * MLSTM_HEADS]
        ut_scr[c] = un.T[:2 * MLSTM_HEADS]
        return carry

    def head_passes(hh):
        hl = slice(hh * d, (hh + 1) * d)
        head = head0 + hh
        sel2 = jnp.where(ch_row == jnp.where(ch_fw, head, MLSTM_HEADS + head), 1.0, 0.0).astype(BF16)

        def intra(c, carry):
            sl = chunk(c)
            q_t = q_ref[0, sl, hl].astype(F32).T.astype(BF16)
            vx_t = jnp.concatenate([v_ref[0, sl, hl].astype(F32).T.astype(BF16), ones_t], axis=0)
            qt_scr[c] = q_t
            vxt_scr[c] = vx_t
            kq = jnp.dot(k_ref[0, sl, hl], q_t, preferred_element_type=F32)
            ucol = sum(jnp.dot(up_scr[i, sl, :], sel2, preferred_element_type=F32)
                       for i in range(3))
            cmu = jnp.max(jnp.where(mask2, ucol, -jnp.inf), axis=0, keepdims=True)
            e0 = jnp.where(mask2, jnp.exp(ucol - cmu), 0.0)
            s0 = (jnp.concatenate([kq, kq], axis=1) * e0).astype(BF16)
            a_scr[c] = jnp.dot(vx_t, s0, preferred_element_type=F32)
            cm_scr[c] = cmu
            return carry

        def recurrence(j, ms):
            out = []
            for dd in range(2):
                c = j if dd == 0 else nc - 1 - j
                tl = slice(dd * L, (dd + 1) * L)
                m = ms[dd]
                row = pl.ds(dd * MLSTM_HEADS + head, 1)
                urow = ut_scr[c, row, :]
                brow = bt_scr[c, row, :]
                cmu = cm_scr[c, :, tl]
                b_last = brow[:, lasts[dd]:lasts[dd] + 1]
                u_max = cmu[:, lasts[dd]:lasts[dd] + 1]
                mm = jnp.maximum(cmu, m)
                cx = c_scr[dd]
                numx = (a_scr[c, :, tl] * jnp.exp(cmu - mm) + jnp.exp(m - mm)
                        * jnp.dot(cx.astype(BF16), qt_scr[c], preferred_element_type=F32))
                den = jnp.maximum(jnp.abs(numx[d:d + 1]), jnp.exp(-(brow + mm)))
                ht_scr[dd, c] = numx[:d] * (1.0 / den)
                m_new = b_last + jnp.maximum(m, u_max)
                wk = jnp.exp(b_last + urow - m_new)
                vw = (vxt_scr[c].astype(F32) * wk).astype(BF16)
                c_scr[dd] = (jnp.exp(b_last + m - m_new) * cx
                             + jnp.dot(vw, k_ref[0, chunk(c), hl], preferred_element_type=F32))
                out.append(m_new)
            return tuple(out)

        def finish(c, carry):
            sl = chunk(c)
            hn = _rms((ht_scr[0, c] + ht_scr[1, c]).T, ng_ref[:, hl])
            y_ref[0, sl, hl] = (hn * jax.nn.sigmoid(o_ref[0, sl, hl].astype(F32))).astype(y_ref.dtype)
            return carry

        c_scr[...] = jnp.zeros_like(c_scr)
        lax.fori_loop(0, nc, intra, 0, unroll=MLSTM_UNROLL)
        lax.fori_loop(0, nc, recurrence, (jnp.zeros((1, 1), F32),) * 2, unroll=MLSTM_UNROLL)
        lax.fori_loop(0, nc, finish, 0, unroll=MLSTM_UNROLL)

    lax.fori_loop(0, nc, gates_pass, 0, unroll=MLSTM_UNROLL)
    for hh in range(heads):
        head_passes(hh)


def _mlstm(z3, gates3, gate_bias, norm_g):
    B, S, _ = z3.shape
    H, hps = MLSTM_HEADS, MLSTM_HEADS_PER_STEP
    L, d = MLSTM_CHUNK, HEAD_DIM
    nc = S // L
    w = hps * d
    rows = d + MLSTM_DEN_ROWS
    col = lambda part: pl.BlockSpec((1, S, w), lambda b, h: (b, 0, part * (H // hps) + h))
    gate = lambda part: pl.BlockSpec((1, S, GATE_LANES), lambda b, h: (b, 0, part))
    bias = lambda part: pl.BlockSpec((1, GATE_LANES), lambda b, h: (0, part))
    return pl.pallas_call(
        _mlstm_kernel,
        grid=(B, H // hps),
        in_specs=[col(0), col(1), col(2), col(3), gate(0), gate(1), bias(0), bias(1),
                  pl.BlockSpec((1, w), lambda b, h: (0, h))],
        out_specs=pl.BlockSpec((1, S, w), lambda b, h: (b, 0, h)),
        out_shape=jax.ShapeDtypeStruct((B, S, MLSTM_W), BF16),
        scratch_shapes=[pltpu.VMEM((3, S, GATE_LANES), BF16),
                        pltpu.VMEM((nc, 2 * H, L), F32),
                        pltpu.VMEM((nc, 2 * H, L), F32),
                        pltpu.VMEM((nc, 1, 2 * L), F32),
                        pltpu.VMEM((nc, d, L), BF16),
                        pltpu.VMEM((nc, rows, L), BF16),
                        pltpu.VMEM((nc, rows, 2 * L), F32),
                        pltpu.VMEM((2, nc, d, L), F32),
                        pltpu.VMEM((2, rows, d), F32)],
        compiler_params=_params(("parallel", "arbitrary")),
        name="mlstm",
    )(z3, z3, z3, z3, gates3, gates3, gate_bias, gate_bias, norm_g)


def _rope(t, c, s_lo, s_hi):
    quarter = HEAD_DIM // 4
    return (t * c + pltpu.roll(t, HEAD_DIM - quarter, 1) * s_lo
            + pltpu.roll(t, quarter, 1) * s_hi)


def _attn_kernel(q_ref, k_ref, v_ref, cq_ref, sloq_ref, shiq_ref, ck_ref, slok_ref, shik_ref,
                 qg_ref, kg_ref, o_ref, k_scr):
    S = k_ref.shape[1]
    step = min(256, S)

    @pl.when(pl.program_id(2) == 0)
    def _():
        def body(c, carry):
            r = pl.ds(pl.multiple_of(c * step, step), step)
            kn = _rms(k_ref[0, r, :].astype(F32), kg_ref[...])
            k_scr[r, :] = _rope(kn, ck_ref[r, :], slok_ref[r, :], shik_ref[r, :]).astype(BF16)
            return carry

        lax.fori_loop(0, S // step, body, 0)

    for g in range(ATTN_GROUP):
        hl = slice(g * HEAD_DIM, (g + 1) * HEAD_DIM)
        qn = _rms(q_ref[0, :, hl].astype(F32), qg_ref[...])
        qr = _rope(qn, cq_ref[...], sloq_ref[...], shiq_ref[...]).astype(BF16)
        s = lax.dot_general(qr, k_scr[...], (((1,), (1,)), ((), ())), preferred_element_type=F32)
        p = jnp.exp2(s - jnp.max(s, axis=1, keepdims=True))
        l = jnp.sum(p, axis=1, keepdims=True)
        o = jnp.dot(p.astype(BF16), v_ref[0], preferred_element_type=F32)
        o_ref[0, :, hl] = (o / l).astype(o_ref.dtype)


def _attention(z3, q_blk0, tables, q_g, k_g, tq):
    B, S, _ = z3.shape
    tq = min(tq, S)
    cos_t, slo_t, shi_t = tables
    gw = ATTN_GROUP * HEAD_DIM
    q_grp0 = q_blk0 // ATTN_GROUP
    assert q_grp0 * ATTN_GROUP == q_blk0
    k_blk0 = q_blk0 + ATTN_HEADS
    v_blk0 = k_blk0 + ATTN_KV_HEADS
    qtab = pl.BlockSpec((tq, HEAD_DIM), lambda b, kv, i: (i, 0))
    ktab = pl.BlockSpec((S, HEAD_DIM), lambda b, kv, i: (0, 0))
    gain = pl.BlockSpec((1, HEAD_DIM), lambda b, kv, i: (0, 0))
    return pl.pallas_call(
        _attn_kernel,
        grid=(B, ATTN_KV_HEADS, S // tq),
        in_specs=[pl.BlockSpec((1, tq, gw), lambda b, kv, i: (b, i, q_grp0 + kv)),
                  pl.BlockSpec((1, S, HEAD_DIM), lambda b, kv, i: (b, 0, k_blk0 + kv)),
                  pl.BlockSpec((1, S, HEAD_DIM), lambda b, kv, i: (b, 0, v_blk0 + kv)),
                  qtab, qtab, qtab, ktab, ktab, ktab, gain, gain],
        out_specs=pl.BlockSpec((1, tq, gw), lambda b, kv, i: (b, i, kv)),
        out_shape=jax.ShapeDtypeStruct((B, S, ATTN_Q_W), BF16),
        scratch_shapes=[pltpu.VMEM((S, HEAD_DIM), BF16)],
        compiler_params=_params(("parallel", "parallel", "arbitrary")),
        name="attention",
    )(z3, z3, z3, cos_t, slo_t, shi_t, cos_t, slo_t, shi_t, q_g, k_g)


def _merge_kernel(ym_ref, ya_ref, gm_ref, ga_ref, wm_ref, wa_ref, o_ref):
    bm = jnp.dot(ym_ref[...], wm_ref[...], preferred_element_type=F32)
    ba = jnp.dot(ya_ref[...], wa_ref[...], preferred_element_type=F32)
    o_ref[...] = (jax.nn.sigmoid(gm_ref[...].astype(F32)) * bm
                  + jax.nn.sigmoid(ga_ref[...].astype(F32)) * ba).astype(o_ref.dtype)


def _merge(y_m, y_a, z, gm_col0, ga_col0, w_m, w_a, tm, tn):
    T, K = y_m.shape
    D = w_m.shape[1]
    tm, tn = min(tm, T), min(tn, D)
    gm0, ga0 = gm_col0 // tn, ga_col0 // tn
    assert gm0 * tn == gm_col0 and ga0 * tn == ga_col0
    return pl.pallas_call(
        _merge_kernel,
        grid=(T // tm, D // tn),
        in_specs=[pl.BlockSpec((tm, K), lambda i, j: (i, 0)),
                  pl.BlockSpec((tm, K), lambda i, j: (i, 0)),
                  pl.BlockSpec((tm, tn), lambda i, j: (i, gm0 + j)),
                  pl.BlockSpec((tm, tn), lambda i, j: (i, ga0 + j)),
                  pl.BlockSpec((K, tn), lambda i, j: (0, j)),
                  pl.BlockSpec((K, tn), lambda i, j: (0, j))],
        out_specs=pl.BlockSpec((tm, tn), lambda i, j: (i, j)),
        out_shape=jax.ShapeDtypeStruct((T, D), BF16),
        compiler_params=_params(("parallel", "arbitrary")),
        name="merge",
    )(y_m, y_a, z, z, w_m, w_a)


def _outproj_kernel(a_ref, w_ref, x_ref, o_ref):
    o_ref[...] = x_ref[...] + jnp.dot(a_ref[...], w_ref[...], preferred_element_type=F32)


def _outproj(a, w, x, tm, tn):
    T, K = a.shape
    D = w.shape[1]
    tm, tn = min(tm, T), min(tn, D)
    return pl.pallas_call(
        _outproj_kernel,
        grid=(T // tm, D // tn),
        in_specs=[pl.BlockSpec((tm, K), lambda i, j: (i, 0)),
                  pl.BlockSpec((K, tn), lambda i, j: (0, j)),
                  pl.BlockSpec((tm, tn), lambda i, j: (i, j))],
        out_specs=pl.BlockSpec((tm, tn), lambda i, j: (i, j)),
        out_shape=jax.ShapeDtypeStruct((T, D), F32),
        compiler_params=_params(("parallel", "arbitrary")),
        name="outproj",
    )(a, w, x)


def _first_argmax(vals, vmax):
    idx = jnp.full(vals[0].shape, len(vals), jnp.int32)
    for j in reversed(range(len(vals))):
        idx = jnp.where(vals[j] == vmax, j, idx)
    return idx


def _pack_bf16_pairs(h):
    n = h.shape[1] // 2
    hb = h.astype(BF16).astype(F32)
    lo = lax.bitcast_convert_type(hb[:, :n], jnp.uint32) >> 16
    hi = lax.bitcast_convert_type(hb[:, n:], jnp.uint32) & jnp.uint32(0xFFFF0000)
    return lo | hi


def _unpack_bf16_pairs(p):
    lo = lax.bitcast_convert_type(p << 16, F32)
    hi = lax.bitcast_convert_type(p & jnp.uint32(0xFFFF0000), F32)
    return jnp.concatenate([lo, hi], axis=1).astype(BF16)


def _router_kernel(x_ref, g_ref, wr_ref, br_ref, idx_ref, wts_ref, cnt_ref, hp_ref,
                   tri_scr, carry_scr):
    tm = x_ref.shape[0]

    @pl.when(pl.program_id(0) == 0)
    def _():
        r = lax.broadcasted_iota(jnp.int32, (tm, tm), 0)
        c = lax.broadcasted_iota(jnp.int32, (tm, tm), 1)
        tri_scr[...] = jnp.where(r < c, 1.0, 0.0).astype(BF16)
        carry_scr[...] = jnp.zeros_like(carry_scr)

    h = _rms(x_ref[...], g_ref[...])
    hp_ref[...] = _pack_bf16_pairs(h)
    logits = lax.dot_general(wr_ref[...], h, (((1,), (1,)), ((), ())),
                             preferred_element_type=F32,
                             precision=lax.Precision.HIGHEST) + br_ref[...]
    gl = [logits[j:j + 1, :] for j in range(N_GROUPS)]
    gmax = functools.reduce(jnp.maximum, gl)
    gi = _first_argmax(gl, gmax)
    gp = 1.0 / functools.reduce(jnp.add, [jnp.exp(v - gmax) for v in gl])
    eg = []
    for e in range(EXPERTS_PER_GROUP):
        v = logits[ROUTER_EXPERT_ROW0 + e:ROUTER_EXPERT_ROW0 + e + 1, :]
        for grp in range(1, N_GROUPS):
            r0 = ROUTER_EXPERT_ROW0 + grp * EXPERTS_PER_GROUP + e
            v = jnp.where(gi == grp, logits[r0:r0 + 1, :], v)
        eg.append(v)
    v1 = functools.reduce(jnp.maximum, eg)
    i1 = _first_argmax(eg, v1)
    eg2 = [jnp.where(i1 == e, -jnp.inf, eg[e]) for e in range(EXPERTS_PER_GROUP)]
    v2 = functools.reduce(jnp.maximum, eg2)
    i2 = _first_argmax(eg2, v2)
    e2 = jnp.exp(v2 - v1)
    w1 = gp / (1.0 + e2)
    w2 = gp * e2 / (1.0 + e2)
    eid1 = gi * EXPERTS_PER_GROUP + i1
    eid2 = gi * EXPERTS_PER_GROUP + i2

    erow = lax.broadcasted_iota(jnp.int32, (N_EXPERTS, tm), 0)
    oh1 = erow == eid1
    oh2 = erow == eid2
    oh = jnp.where(oh1 | oh2, 1.0, 0.0)
    before = jnp.dot(oh.astype(BF16), tri_scr[...], preferred_element_type=F32) + carry_scr[...]
    rank1 = jnp.sum(jnp.where(oh1, before, 0.0), axis=0, keepdims=True)
    rank2 = jnp.sum(jnp.where(oh2, before, 0.0), axis=0, keepdims=True)
    carry_scr[...] = carry_scr[...] + jnp.sum(oh, axis=1, keepdims=True)

    idx_ref[...] = jnp.zeros_like(idx_ref)
    idx_ref[0:1, :] = eid1
    idx_ref[1:2, :] = eid2
    idx_ref[2:3, :] = rank1.astype(jnp.int32)
    idx_ref[3:4, :] = rank2.astype(jnp.int32)
    wts_ref[...] = jnp.zeros_like(wts_ref)
    wts_ref[0:1, :] = w1
    wts_ref[1:2, :] = w2
    cnt_ref[...] = jnp.broadcast_to(carry_scr[...], cnt_ref.shape).astype(jnp.int32)


def _router(x, g, wr, br, tm):
    T, D = x.shape
    tm = min(tm, T)
    return pl.pallas_call(
        _router_kernel,
        grid=(T // tm,),
        in_specs=[pl.BlockSpec((tm, D), lambda i: (i, 0)),
                  pl.BlockSpec((1, D), lambda i: (0, 0)),
                  pl.BlockSpec((ROUTER_ROWS, D), lambda i: (0, 0)),
                  pl.BlockSpec((ROUTER_ROWS, 1), lambda i: (0, 0))],
        out_specs=[pl.BlockSpec((8, tm), lambda i: (0, i)),
                   pl.BlockSpec((8, tm), lambda i: (0, i)),
                   pl.BlockSpec((N_EXPERTS, 128), lambda i: (0, 0)),
                   pl.BlockSpec((tm, D // 2), lambda i: (i, 0))],
        out_shape=[jax.ShapeDtypeStruct((8, T), jnp.int32),
                   jax.ShapeDtypeStruct((8, T), F32),
                   jax.ShapeDtypeStruct((N_EXPERTS, 128), jnp.int32),
                   jax.ShapeDtypeStruct((T, D // 2), jnp.uint32)],
        scratch_shapes=[pltpu.VMEM((tm, tm), BF16), pltpu.VMEM((N_EXPERTS, 1), F32)],
        compiler_params=_params(("arbitrary",)),
        name="router",
    )(x, g, wr, br)


def _slots_kernel(idx_ref, cnt_ref, dest_ref, blk_ref):
    shift = MOE_BLOCK.bit_length() - 1
    cnt = cnt_ref[...]
    padded = ((cnt + (MOE_BLOCK - 1)) >> shift) << shift
    eid1, eid2 = idx_ref[0:1, :], idx_ref[1:2, :]
    d1, d2 = idx_ref[2:3, :], idx_ref[3:4, :]
    blk_start = lax.broadcasted_iota(jnp.int32, (1, blk_ref.shape[1]), 1) * MOE_BLOCK
    blk_e = jnp.zeros_like(blk_start)
    acc = jnp.zeros((1, 1), jnp.int32)
    for e in range(N_EXPERTS):
        d1 = d1 + jnp.where(eid1 == e, acc, 0)
        d2 = d2 + jnp.where(eid2 == e, acc, 0)
        acc = acc + padded[e:e + 1, 0:1]
        blk_e = blk_e + jnp.where(acc <= blk_start, 1, 0)
    dest_ref[...] = jnp.zeros_like(dest_ref)
    dest_ref[0:1, :] = d1
    dest_ref[1:2, :] = d2
    blk_ref[...] = jnp.zeros_like(blk_ref)
    blk_ref[0:1, :] = jnp.minimum(blk_e, N_EXPERTS - 1)


def _slots(idx, cnt, n_blk, tm):
    T = idx.shape[1]
    tm = min(tm, T)
    lanes = -(-n_blk // 128) * 128
    return pl.pallas_call(
        _slots_kernel,
        grid=(T // tm,),
        in_specs=[pl.BlockSpec((8, tm), lambda i: (0, i)),
                  pl.BlockSpec((N_EXPERTS, 128), lambda i: (0, 0))],
        out_specs=[pl.BlockSpec((8, tm), lambda i: (0, i)),
                   pl.BlockSpec((8, lanes), lambda i: (0, 0))],
        out_shape=[jax.ShapeDtypeStruct((8, T), jnp.int32),
                   jax.ShapeDtypeStruct((8, lanes), jnp.int32)],
        compiler_params=_params(("arbitrary",)),
        name="slots",
    )(idx, cnt)


def _invert_kernel(dest_ref, slot_ref, *, n_tok, slot_chunk, tok_chunk):
    phase, step = pl.program_id(0), pl.program_id(1)

    @pl.when(phase == 0)
    def _():
        def zero(s, carry):
            slot_ref[step * slot_chunk + s] = 0
            return carry
        lax.fori_loop(0, slot_chunk, zero, 0, unroll=8)

    @pl.when(phase == 1)
    def _():
        def put(r, carry):
            t = step * tok_chunk + r
            slot_ref[dest_ref[t]] = t
            slot_ref[dest_ref[n_tok + t]] = t
            return carry
        lax.fori_loop(0, tok_chunk, put, 0, unroll=8)


def _invert(dest_flat, n_slots, n_tok, steps):
    while n_slots % steps or n_tok % steps:
        steps //= 2
    return pl.pallas_call(
        functools.partial(_invert_kernel, n_tok=n_tok, slot_chunk=n_slots // steps,
                          tok_chunk=n_tok // steps),
        grid=(2, steps),
        in_specs=[pl.BlockSpec(memory_space=pltpu.SMEM)],
        out_specs=pl.BlockSpec(memory_space=pltpu.SMEM),
        out_shape=jax.ShapeDtypeStruct((n_slots,), jnp.int32),
        compiler_params=_params(("arbitrary", "arbitrary")),
        name="invert",
    )(dest_flat)


def _expert_kernel(be_ref, tok_ref, hp_hbm, wg_ref, wu_ref, wd_ref, ys_ref,
                   xbuf, wg_scr, wu_scr, wd_scr, sem):
    i = pl.program_id(0)
    last = pl.num_programs(0) - 1

    def row_copy(blk, buf, r):
        t = tok_ref[blk * MOE_BLOCK + r]
        return pltpu.make_async_copy(hp_hbm.at[pl.ds(t, 1)], xbuf.at[buf, pl.ds(r, 1)], sem.at[buf])

    def all_rows(fn):
        def body(r, carry):
            fn(r)
            return carry
        lax.fori_loop(0, MOE_BLOCK, body, 0, unroll=8)

    @pl.when(i == 0)
    def _():
        all_rows(lambda r: row_copy(0, 0, r).start())

    all_rows(lambda r: row_copy(i, i % 2, r).wait())

    @pl.when((i == 0) | (be_ref[i] != be_ref[jnp.maximum(i - 1, 0)]))
    def _():
        def cast(src, dst):
            step = 256
            def body(c, carry):
                r = pl.ds(pl.multiple_of(c * step, step), step)
                dst[r, :] = src[0, r, :].astype(BF16)
                return carry
            lax.fori_loop(0, dst.shape[0] // step, body, 0)
        cast(wg_ref, wg_scr)
        cast(wu_ref, wu_scr)
        cast(wd_ref, wd_scr)

    nxt = jnp.minimum(i + 1, last)
    for r in range(MOE_BLOCK):
        row_copy(nxt, (i + 1) % 2, r).start(priority=r % 2)
    h = _unpack_bf16_pairs(xbuf[i % 2])
    a = jnp.dot(h, wg_scr[...], preferred_element_type=F32)
    u = jnp.dot(h, wu_scr[...], preferred_element_type=F32)
    act = (a * jax.nn.sigmoid(a) * u).astype(BF16)
    ys_ref[...] = jnp.dot(act, wd_scr[...], preferred_element_type=F32)

    @pl.when(i == last)
    def _():
        all_rows(lambda r: row_copy(nxt, (i + 1) % 2, r).wait())


def _experts(blk_e, slot_tok, hp, w_g, w_u, w_d):
    _, D, FF = w_g.shape
    n_blk = blk_e.shape[0]
    return pl.pallas_call(
        _expert_kernel,
        grid_spec=pltpu.PrefetchScalarGridSpec(
            num_scalar_prefetch=2,
            grid=(n_blk,),
            in_specs=[pl.BlockSpec(memory_space=pl.ANY),
                      pl.BlockSpec((1, D, FF), lambda i, be, tok: (be[i], 0, 0)),
                      pl.BlockSpec((1, D, FF), lambda i, be, tok: (be[i], 0, 0)),
                      pl.BlockSpec((1, FF, D), lambda i, be, tok: (be[i], 0, 0))],
            out_specs=pl.BlockSpec((MOE_BLOCK, D), lambda i, be, tok: (i, 0)),
            scratch_shapes=[pltpu.VMEM((2, MOE_BLOCK, D // 2), jnp.uint32),
                            pltpu.VMEM((D, FF), BF16), pltpu.VMEM((D, FF), BF16),
                            pltpu.VMEM((FF, D), BF16),
                            pltpu.SemaphoreType.DMA((2,))]),
        out_shape=jax.ShapeDtypeStruct((n_blk * MOE_BLOCK, D), F32),
        compiler_params=_params(("arbitrary",)),
        name="experts",
    )(blk_e, slot_tok, hp, w_g, w_u, w_d)


def _combine_kernel(dest_ref, x_ref, w_ref, ys_hbm, o_ref, ybuf, sem, *, n_tok):
    tm = x_ref.shape[0]
    i = pl.program_id(0)

    def row_copy(step, r, k):
        d = dest_ref[k * n_tok + step * tm + r]
        buf = step % 2
        return pltpu.make_async_copy(ys_hbm.at[pl.ds(d, 1)], ybuf.at[buf, k, pl.ds(r, 1)],
                                     sem.at[buf])

    def start_step(step):
        def body(r, carry):
            row_copy(step, r, 0).start(priority=0)
            row_copy(step, r, 1).start(priority=1)
            return carry
        lax.fori_loop(0, tm, body, 0, unroll=8)

    @pl.when(i == 0)
    def _():
        start_step(0)

    @pl.when(i + 1 < pl.num_programs(0))
    def _():
        start_step(i + 1)

    def wait(r, carry):
        row_copy(i, r, 0).wait()
        row_copy(i, r, 1).wait()
        return carry

    lax.fori_loop(0, tm, wait, 0, unroll=8)
    w = w_ref[...]
    o_ref[...] = x_ref[...] + w[:, 0:1] * ybuf[i % 2, 0] + w[:, 1:2] * ybuf[i % 2, 1]


def _combine(dest_flat, x, w_cols, ys, tm):
    T, D = x.shape
    tm = min(tm, T)
    return pl.pallas_call(
        functools.partial(_combine_kernel, n_tok=T),
        grid_spec=pltpu.PrefetchScalarGridSpec(
            num_scalar_prefetch=1,
            grid=(T // tm,),
            in_specs=[pl.BlockSpec((tm, D), lambda i, d: (i, 0)),
                      pl.BlockSpec((tm, 2), lambda i, d: (i, 0)),
                      pl.BlockSpec(memory_space=pl.ANY)],
            out_specs=pl.BlockSpec((tm, D), lambda i, d: (i, 0)),
            scratch_shapes=[pltpu.VMEM((2, 2, tm, D), F32), pltpu.SemaphoreType.DMA((2,))]),
        out_shape=jax.ShapeDtypeStruct((T, D), F32),
        compiler_params=_params(("arbitrary",)),
        name="combine",
    )(dest_flat, x, w_cols, ys)


def _ple_kernel(x_ref, g_ref, p_ref, wg_ref, wp_ref, o_ref, h_scr):
    tn = o_ref.shape[1]
    j = pl.program_id(1)

    @pl.when(j == 0)
    def _():
        _norm_rows_into(x_ref, g_ref, h_scr)

    gate = jax.nn.sigmoid(jnp.dot(h_scr[...], wg_ref[...], preferred_element_type=F32))
    proj = jnp.dot(p_ref[...].astype(BF16), wp_ref[...], preferred_element_type=F32)
    o_ref[...] = x_ref[:, pl.ds(pl.multiple_of(j * tn, tn), tn)] + gate * proj


def _ple(x, g, p, w_g, w_p, tm, tn):
    T, D = x.shape
    PD = p.shape[1]
    tm, tn = min(tm, T), min(tn, D)
    return pl.pallas_call(
        _ple_kernel,
        grid=(T // tm, D // tn),
        in_specs=[pl.BlockSpec((tm, D), lambda i, j: (i, 0)),
                  pl.BlockSpec((1, D), lambda i, j: (0, 0)),
                  pl.BlockSpec((tm, PD), lambda i, j: (i, 0)),
                  pl.BlockSpec((D, tn), lambda i, j: (0, j)),
                  pl.BlockSpec((PD, tn), lambda i, j: (0, j))],
        out_specs=pl.BlockSpec((tm, tn), lambda i, j: (i, j)),
        out_shape=jax.ShapeDtypeStruct((T, D), F32),
        scratch_shapes=[pltpu.VMEM((tm, D), BF16)],
        compiler_params=_params(("parallel", "arbitrary")),
        name="ple",
    )(x, g, p, w_g, w_p)


def _final_norm_kernel(x_ref, g_ref, o_ref):
    o_ref[...] = _rms(x_ref[...], g_ref[...])


def _final_norm(x, g, tm):
    T, D = x.shape
    tm = min(tm, T)
    return pl.pallas_call(
        _final_norm_kernel,
        grid=(T // tm,),
        in_specs=[pl.BlockSpec((tm, D), lambda i: (i, 0)), pl.BlockSpec((1, D), lambda i: (0, 0))],
        out_specs=pl.BlockSpec((tm, D), lambda i: (i, 0)),
        out_shape=jax.ShapeDtypeStruct((T, D), F32),
        compiler_params=_params(("parallel",)),
        name="final_norm",
    )(x, g)


def _rope_tables(S):
    pos = np.arange(S)
    inv_freq = ROPE_THETA ** (-np.arange(ROPE_PAIRS, dtype=np.float32) / ROPE_PAIRS)
    ang_row = jnp.asarray((pos // GRID_W).astype(np.float32)[:, None] * inv_freq[None, :])
    ang_col = jnp.asarray((pos % GRID_W).astype(np.float32)[:, None] * inv_freq[None, :])
    zero = jnp.zeros_like(ang_row)
    cos_t = jnp.concatenate([jnp.cos(ang_row)] * 2 + [jnp.cos(ang_col)] * 2, axis=1)
    sin_lo = jnp.concatenate([-jnp.sin(ang_row), zero, -jnp.sin(ang_col), zero], axis=1)
    sin_hi = jnp.concatenate([zero, jnp.sin(ang_row), zero, jnp.sin(ang_col)], axis=1)
    return cos_t, sin_lo, sin_hi


def kernel(x, p, norm1_g, w_in, mlstm_gate_b, mlstm_norm_g, q_norm_g, k_norm_g, w_branch_m, w_branch_a, w_out, norm2_g, w_router_group, b_router_group, w_router_expert, b_router_expert, w_exp_gate, w_exp_up, w_exp_down, ple_norm_g, w_ple_gate, w_ple_proj, final_norm_g):
    B, S, D = x.shape
    depth = w_in.shape[0]
    T = B * S
    H = MLSTM_HEADS
    scale = HEAD_DIM ** -0.5
    tables = _rope_tables(S)
    row = lambda v: v.reshape(1, -1).astype(F32)

    o_qm, o_gates = 0, 4 * MLSTM_W
    o_qa = o_gates + 4 * H
    o_gm = o_qa + ATTN_Q_W + 2 * ATTN_KV_W
    o_ga = o_gm + D
    z_qa = 4 * MLSTM_W
    z_gm = z_qa + ATTN_Q_W + 2 * ATTN_KV_W
    z_ga = z_gm + D

    xf = x.reshape(T, D)
    for i in range(depth):
        wi = w_in[i]
        w_main = jnp.concatenate([wi[:, o_qm:o_qm + MLSTM_W] * scale, wi[:, MLSTM_W:o_gates],
                                  wi[:, o_qa:]], axis=1).astype(BF16)
        lane_pad = ((0, 0), (0, GATE_LANES - 2 * H))
        w_gates = jnp.concatenate([jnp.pad(wi[:, o_gates:o_gates + 2 * H], lane_pad),
                                   jnp.pad(wi[:, o_gates + 2 * H:o_qa], lane_pad)], axis=1).astype(BF16)
        gb = mlstm_gate_b[i].astype(F32).reshape(1, 4 * H)
        gate_bias = jnp.concatenate([jnp.pad(gb[:, :2 * H], lane_pad),
                                     jnp.pad(gb[:, 2 * H:], lane_pad)], axis=1)
        z, gates = _inproj(xf, row(norm1_g[i]), w_main, w_gates, 1024, 512)
        z3 = z.reshape(B, S, -1)
        y_m = _mlstm(z3, gates.reshape(B, S, -1), gate_bias, row(mlstm_norm_g[i]))
        y_a = _attention(z3, z_qa // HEAD_DIM, tables, row(q_norm_g[i]) * (scale * np.log2(np.e)),
                         row(k_norm_g[i]), 256)

        merged = _merge(y_m.reshape(T, MLSTM_W), y_a.reshape(T, ATTN_Q_W), z, z_gm, z_ga,
                        w_branch_m[i].astype(BF16), w_branch_a[i].astype(BF16), 1024, 512)
        xf = _outproj(merged, w_out[i].astype(BF16), xf, 1024, 512)

        wr = jnp.zeros((ROUTER_ROWS, D), F32)
        wr = wr.at[:N_GROUPS].set(w_router_group[i].T)
        wr = wr.at[ROUTER_EXPERT_ROW0:ROUTER_EXPERT_ROW0 + N_EXPERTS].set(w_router_expert[i].T)
        br = jnp.zeros((ROUTER_ROWS, 1), F32)
        br = br.at[:N_GROUPS, 0].set(b_router_group[i])
        br = br.at[ROUTER_EXPERT_ROW0:ROUTER_EXPERT_ROW0 + N_EXPERTS, 0].set(b_router_expert[i])
        idx, wts, cnt, hp = _router(xf, row(norm2_g[i]), wr, br, 512)
        n_blk = -(-2 * T // MOE_BLOCK) + N_EXPERTS
        dest8, blk = _slots(idx, cnt, n_blk, 4096)
        dest = dest8[0:2].reshape(-1)
        slot_tok = _invert(dest, n_blk * MOE_BLOCK, T, 64)
        ys = _experts(blk[0, :n_blk], slot_tok, hp, w_exp_gate[i], w_exp_up[i], w_exp_down[i])
        xf = _combine(dest, xf, wts[0:2].T, ys, 256)

        xf = _ple(xf, row(ple_norm_g[i]), p[i].reshape(T, -1), w_ple_gate[i].astype(BF16),
                  w_ple_proj[i].astype(BF16), 1024, 512)

    return _final_norm(xf, row(final_norm_g), 512).reshape(B, S, D)
```

```python
import functools

import jax
import jax.numpy as jnp
import numpy as np
from jax import lax
from jax.experimental import pallas as pl
from jax.experimental.pallas import tpu as pltpu

F32 = jnp.float32
BF16 = jnp.bfloat16

GRID_W = 64
HEAD_DIM = 128
MLSTM_HEADS = 8
MLSTM_W = MLSTM_HEADS * HEAD_DIM
MLSTM_CHUNK = 128
MLSTM_HEADS_PER_STEP = 4
MLSTM_UNROLL = 4
MLSTM_DEN_ROWS = 16
ATTN_HEADS = 8
ATTN_KV_HEADS = 2
ATTN_GROUP = ATTN_HEADS // ATTN_KV_HEADS
ATTN_Q_W = ATTN_HEADS * HEAD_DIM
ATTN_KV_W = ATTN_KV_HEADS * HEAD_DIM
ROPE_THETA = 10000.0
ROPE_PAIRS = HEAD_DIM // 4
N_GROUPS = 4
EXPERTS_PER_GROUP = 8
N_EXPERTS = N_GROUPS * EXPERTS_PER_GROUP
MOE_BLOCK = 256
EPS = 1e-6
GATE_LANES = 128
ROUTER_ROWS = 128
ROUTER_EXPERT_ROW0 = 8
VMEM_LIMIT_BYTES = 52 * 1024 * 1024
NORM_ROWS = 128


def _params(semantics, **kw):
    return pltpu.CompilerParams(dimension_semantics=semantics,
                                vmem_limit_bytes=VMEM_LIMIT_BYTES, **kw)


def _rms(x, g):
    ms = jnp.mean(x * x, axis=-1, keepdims=True)
    return (x * lax.rsqrt(ms + EPS)) * g


def _norm_rows_into(x_ref, g_ref, h_scr):
    rows = x_ref.shape[0]
    step = min(NORM_ROWS, rows)

    def body(c, carry):
        r = pl.ds(pl.multiple_of(c * step, step), step)
        h_scr[r, :] = _rms(x_ref[r, :], g_ref[...]).astype(h_scr.dtype)
        return carry

    lax.fori_loop(0, rows // step, body, 0)


def _norm_cast_kernel(x_ref, g_ref, h_ref):
    h_ref[...] = _rms(x_ref[...], g_ref[...]).astype(h_ref.dtype)


def _norm_cast(x, g, tm):
    T, D = x.shape
    tm = min(tm, T)
    return pl.pallas_call(
        _norm_cast_kernel,
        grid=(T // tm,),
        in_specs=[pl.BlockSpec((tm, D), lambda i: (i, 0)), pl.BlockSpec((1, D), lambda i: (0, 0))],
        out_specs=pl.BlockSpec((tm, D), lambda i: (i, 0)),
        out_shape=jax.ShapeDtypeStruct((T, D), BF16),
        compiler_params=_params(("parallel",)),
        name="norm_cast",
    )(x, g)


def _matmul_kernel(h_ref, w_ref, o_ref):
    o_ref[...] = jnp.dot(h_ref[...], w_ref[...], preferred_element_type=F32).astype(o_ref.dtype)


def _inproj(h, w, col_groups, tm):
    T, D = h.shape
    N = w.shape[1]
    tm = min(tm, T)
    tn = N // col_groups
    assert tn * col_groups == N and tn % 128 == 0
    return pl.pallas_call(
        _matmul_kernel,
        grid=(col_groups, T // tm),
        in_specs=[pl.BlockSpec((tm, D), lambda j, i: (i, 0)),
                  pl.BlockSpec((D, tn), lambda j, i: (0, j))],
        out_specs=pl.BlockSpec((tm, tn), lambda j, i: (i, j)),
        out_shape=jax.ShapeDtypeStruct((T, N), BF16),
        compiler_params=_params(("arbitrary", "arbitrary")),
        name="inproj",
    )(h, w)


def _gates_proj(h, wg, tm):
    T, D = h.shape
    N = wg.shape[1]
    tm = min(tm, T)
    return pl.pallas_call(
        _matmul_kernel,
        grid=(T // tm,),
        in_specs=[pl.BlockSpec((tm, D), lambda i: (i, 0)), pl.BlockSpec((D, N), lambda i: (0, 0))],
        out_specs=pl.BlockSpec((tm, N), lambda i: (i, 0)),
        out_shape=jax.ShapeDtypeStruct((T, N), F32),
        compiler_params=_params(("parallel",)),
        name="gates_proj",
    )(h, wg)


def _log_sigmoid(x):
    return jnp.minimum(x, 0.0) - jnp.log1p(jnp.exp(-jnp.abs(x)))


def _mlstm_kernel(q_ref, k_ref, v_ref, o_ref, gi_ref, gf_ref, bi_ref, bf_ref, ng_ref, y_ref,
                  up_scr, bt_scr, ut_scr, cm_scr, qt_scr, vxt_scr, a_scr, ht_scr, c_scr):
    L, d = MLSTM_CHUNK, HEAD_DIM
    S = q_ref.shape[1]
    nc = S // L
    heads = q_ref.shape[2] // d
    head0 = pl.program_id(1) * heads
    srow = lax.broadcasted_iota(jnp.int32, (L, L), 0)
    tcol = lax.broadcasted_iota(jnp.int32, (L, L), 1)
    mask2 = jnp.concatenate([srow <= tcol, srow >= tcol], axis=1)
    lasts = (L - 1, 0)
    tri2 = jnp.concatenate([jnp.where(tcol <= srow, 1.0, 0.0),
                            jnp.where(tcol >= srow, 1.0, 0.0)], axis=0).astype(BF16)
    fw_lane = lax.broadcasted_iota(jnp.int32, (L, GATE_LANES), 1) < MLSTM_HEADS
    ones_t = jnp.ones((MLSTM_DEN_ROWS, L), BF16)
    ch_row = lax.broadcasted_iota(jnp.int32, (GATE_LANES, 2 * L), 0)
    ch_fw = lax.broadcasted_iota(jnp.int32, (GATE_LANES, 2 * L), 1) < L

    def split3(x):
        p1 = x.astype(BF16)
        r1 = x - p1.astype(F32)
        p2 = r1.astype(BF16)
        p3 = (r1 - p2.astype(F32)).astype(BF16)
        return p1, p2, p3

    def chunk(c):
        return pl.ds(pl.multiple_of(c * L, L), L)

    def gates_pass(c, carry):
        sl = chunk(c)
        ls = _log_sigmoid(gf_ref[0, sl, :] + bf_ref[...])
        cs = sum(jnp.dot(tri2, part, preferred_element_type=F32) for part in split3(ls))
        bn = jnp.where(fw_lane, cs[:L], cs[L:])
        un = gi_ref[0, sl, :] + bi_ref[...] - bn
        for i, part in enumerate(split3(un)):
            up_scr[i, sl, :] = part
        bt_scr[c] = bn.T[:2 * MLSTM_HEADS]
        ut_scr[c] = un.T[:2 * MLSTM_HEADS]
        return carry

    def head_passes(hh):
        hl = slice(hh * d, (hh + 1) * d)
        head = head0 + hh
        sel2 = jnp.where(ch_row == jnp.where(ch_fw, head, MLSTM_HEADS + head), 1.0, 0.0).astype(BF16)

        def intra(c, carry):
            sl = chunk(c)
            q_t = q_ref[0, sl, hl].astype(F32).T.astype(BF16)
            vx_t = jnp.concatenate([v_ref[0, sl, hl].astype(F32).T.astype(BF16), ones_t], axis=0)
            qt_scr[c] = q_t
            vxt_scr[c] = vx_t
            kq = jnp.dot(k_ref[0, sl, hl], q_t, preferred_element_type=F32)
            ucol = sum(jnp.dot(up_scr[i, sl, :], sel2, preferred_element_type=F32)
                       for i in range(3))
            cmu = jnp.max(jnp.where(mask2, ucol, -jnp.inf), axis=0, keepdims=True)
            e0 = jnp.where(mask2, jnp.exp(ucol - cmu), 0.0)
            s0 = (jnp.concatenate([kq, kq], axis=1) * e0).astype(BF16)
            a_scr[c] = jnp.dot(vx_t, s0, preferred_element_type=F32)
            cm_scr[c] = cmu
            return carry

        def recurrence(j, ms):
            out = []
            for dd in range(2):
                c = j if dd == 0 else nc - 1 - j
                tl = slice(dd * L, (dd + 1) * L)
                m = ms[dd]
                row = pl.ds(dd * MLSTM_HEADS + head, 1)
                urow = ut_scr[c, row, :]
                brow = bt_scr[c, row, :]
                cmu = cm_scr[c, :, tl]
                b_last = brow[:, lasts[dd]:lasts[dd] + 1]
                u_max = cmu[:, lasts[dd]:lasts[dd] + 1]
                mm = jnp.maximum(cmu, m)
                cx = c_scr[dd]
                numx = (a_scr[c, :, tl] * jnp.exp(cmu - mm) + jnp.exp(m - mm)
                        * jnp.dot(cx.astype(BF16), qt_scr[c], preferred_element_type=F32))
                den = jnp.maximum(jnp.abs(numx[d:d + 1]), jnp.exp(-(brow + mm)))
                ht_scr[dd, c] = numx[:d] * (1.0 / den)
                m_new = b_last + jnp.maximum(m, u_max)
                wk = jnp.exp(b_last + urow - m_new)
                vw = (vxt_scr[c].astype(F32) * wk).astype(BF16)
                c_scr[dd] = (jnp.exp(b_last + m - m_new) * cx
                             + jnp.dot(vw, k_ref[0, chunk(c), hl], preferred_element_type=F32))
                out.append(m_new)
            return tuple(out)

        def finish(c, carry):
            sl = chunk(c)
            hn = _rms((ht_scr[0, c] + ht_scr[1, c]).T, ng_ref[:, hl])
            y_ref[0, sl, hl] = (hn * jax.nn.sigmoid(o_ref[0, sl, hl].astype(F32))).astype(y_ref.dtype)
            return carry

        c_scr[...] = jnp.zeros_like(c_scr)
        lax.fori_loop(0, nc, intra, 0, unroll=MLSTM_UNROLL)
        lax.fori_loop(0, nc, recurrence, (jnp.zeros((1, 1), F32),) * 2, unroll=MLSTM_UNROLL)
        lax.fori_loop(0, nc, finish, 0, unroll=MLSTM_UNROLL)

    lax.fori_loop(0, nc, gates_pass, 0, unroll=MLSTM_UNROLL)
    for hh in range(heads):
        head_passes(hh)


def _mlstm(z3, col0, gates3, gate_bias, norm_g):
    B, S, _ = z3.shape
    H, hps = MLSTM_HEADS, MLSTM_HEADS_PER_STEP
    L, d = MLSTM_CHUNK, HEAD_DIM
    nc = S // L
    w = hps * d
    rows = d + MLSTM_DEN_ROWS
    blk0 = col0 // w
    assert blk0 * w == col0
    col = lambda part: pl.BlockSpec((1, S, w), lambda b, h: (b, 0, blk0 + part * (H // hps) + h))
    gate = lambda part: pl.BlockSpec((1, S, GATE_LANES), lambda b, h: (b, 0, part))
    bias = lambda part: pl.BlockSpec((1, GATE_LANES), lambda b, h: (0, part))
    return pl.pallas_call(
        _mlstm_kernel,
        grid=(B, H // hps),
        in_specs=[col(0), col(1), col(2), col(3), gate(0), gate(1), bias(0), bias(1),
                  pl.BlockSpec((1, w), lambda b, h: (0, h))],
        out_specs=pl.BlockSpec((1, S, w), lambda b, h: (b, 0, h)),
        out_shape=jax.ShapeDtypeStruct((B, S, MLSTM_W), BF16),
        scratch_shapes=[pltpu.VMEM((3, S, GATE_LANES), BF16),
                        pltpu.VMEM((nc, 2 * H, L), F32),
                        pltpu.VMEM((nc, 2 * H, L), F32),
                        pltpu.VMEM((nc, 1, 2 * L), F32),
                        pltpu.VMEM((nc, d, L), BF16),
                        pltpu.VMEM((nc, rows, L), BF16),
                        pltpu.VMEM((nc, rows, 2 * L), F32),
                        pltpu.VMEM((2, nc, d, L), F32),
                        pltpu.VMEM((2, rows, d), F32)],
        compiler_params=_params(("parallel", "arbitrary")),
        name="mlstm",
    )(z3, z3, z3, z3, gates3, gates3, gate_bias, gate_bias, norm_g)


def _rope(t, c, s_lo, s_hi):
    quarter = HEAD_DIM // 4
    return (t * c + pltpu.roll(t, HEAD_DIM - quarter, 1) * s_lo
            + pltpu.roll(t, quarter, 1) * s_hi)


def _attn_kernel(q_ref, k_ref, v_ref, cq_ref, sloq_ref, shiq_ref, ck_ref, slok_ref, shik_ref,
                 qg_ref, kg_ref, o_ref, k_scr):
    S = k_ref.shape[1]
    step = min(256, S)

    @pl.when(pl.program_id(2) == 0)
    def _():
        def body(c, carry):
            r = pl.ds(pl.multiple_of(c * step, step), step)
            kn = _rms(k_ref[0, r, :].astype(F32), kg_ref[...])
            k_scr[r, :] = _rope(kn, ck_ref[r, :], slok_ref[r, :], shik_ref[r, :]).astype(BF16)
            return carry

        lax.fori_loop(0, S // step, body, 0)

    for g in range(ATTN_GROUP):
        hl = slice(g * HEAD_DIM, (g + 1) * HEAD_DIM)
        qn = _rms(q_ref[0, :, hl].astype(F32), qg_ref[...])
        qr = _rope(qn, cq_ref[...], sloq_ref[...], shiq_ref[...]).astype(BF16)
        s = lax.dot_general(qr, k_scr[...], (((1,), (1,)), ((), ())), preferred_element_type=F32)
        p = jnp.exp2(s - jnp.max(s, axis=1, keepdims=True))
        l = jnp.sum(p, axis=1, keepdims=True)
        o = jnp.dot(p.astype(BF16), v_ref[0], preferred_element_type=F32)
        o_ref[0, :, hl] = (o / l).astype(o_ref.dtype)


def _attention(z3, q_blk0, tables, q_g, k_g, tq):
    B, S, _ = z3.shape
    tq = min(tq, S)
    cos_t, slo_t, shi_t = tables
    gw = ATTN_GROUP * HEAD_DIM
    q_grp0 = q_blk0 // ATTN_GROUP
    assert q_grp0 * ATTN_GROUP == q_blk0
    k_blk0 = q_blk0 + ATTN_HEADS
    v_blk0 = k_blk0 + ATTN_KV_HEADS
    qtab = pl.BlockSpec((tq, HEAD_DIM), lambda b, kv, i: (i, 0))
    ktab = pl.BlockSpec((S, HEAD_DIM), lambda b, kv, i: (0, 0))
    gain = pl.BlockSpec((1, HEAD_DIM), lambda b, kv, i: (0, 0))
    return pl.pallas_call(
        _attn_kernel,
        grid=(B, ATTN_KV_HEADS, S // tq),
        in_specs=[pl.BlockSpec((1, tq, gw), lambda b, kv, i: (b, i, q_grp0 + kv)),
                  pl.BlockSpec((1, S, HEAD_DIM), lambda b, kv, i: (b, 0, k_blk0 + kv)),
                  pl.BlockSpec((1, S, HEAD_DIM), lambda b, kv, i: (b, 0, v_blk0 + kv)),
                  qtab, qtab, qtab, ktab, ktab, ktab, gain, gain],
        out_specs=pl.BlockSpec((1, tq, gw), lambda b, kv, i: (b, i, kv)),
        out_shape=jax.ShapeDtypeStruct((B, S, ATTN_Q_W), BF16),
        scratch_shapes=[pltpu.VMEM((S, HEAD_DIM), BF16)],
        compiler_params=_params(("parallel", "parallel", "arbitrary")),
        name="attention",
    )(z3, z3, z3, cos_t, slo_t, shi_t, cos_t, slo_t, shi_t, q_g, k_g)


def _mix_kernel(ym_ref, ya_ref, gm_ref, ga_ref, wm_ref, wa_ref, wo_ref, x_ref, o_ref, mg_scr, *, tn):
    D = x_ref.shape[1]
    for j in range(D // tn):
        cols = slice(j * tn, (j + 1) * tn)
        bm = jnp.dot(ym_ref[...], wm_ref[:, cols], preferred_element_type=F32)
        ba = jnp.dot(ya_ref[...], wa_ref[:, cols], preferred_element_type=F32)
        mg_scr[:, cols] = (jax.nn.sigmoid(gm_ref[:, cols].astype(F32)) * bm
                           + jax.nn.sigmoid(ga_ref[:, cols].astype(F32)) * ba).astype(mg_scr.dtype)
    for j in range(D // tn):
        cols = slice(j * tn, (j + 1) * tn)
        o_ref[:, cols] = x_ref[:, cols] + jnp.dot(mg_scr[...], wo_ref[:, cols],
                                                  preferred_element_type=F32)


def _mix(y_m, y_a, z, gm_blk, ga_blk, w_m, w_a, w_o, x, tm, tn):
    T, K = y_m.shape
    D = x.shape[1]
    tm, tn = min(tm, T), min(tn, D)
    return pl.pallas_call(
        functools.partial(_mix_kernel, tn=tn),
        grid=(T // tm,),
        in_specs=[pl.BlockSpec((tm, K), lambda i: (i, 0)),
                  pl.BlockSpec((tm, K), lambda i: (i, 0)),
                  pl.BlockSpec((tm, D), lambda i: (i, gm_blk)),
                  pl.BlockSpec((tm, D), lambda i: (i, ga_blk)),
                  pl.BlockSpec((K, D), lambda i: (0, 0)),
                  pl.BlockSpec((K, D), lambda i: (0, 0)),
                  pl.BlockSpec((D, D), lambda i: (0, 0)),
                  pl.BlockSpec((tm, D), lambda i: (i, 0))],
        out_specs=pl.BlockSpec((tm, D), lambda i: (i, 0)),
        out_shape=jax.ShapeDtypeStruct((T, D), F32),
        scratch_shapes=[pltpu.VMEM((tm, D), BF16)],
        compiler_params=_params(("parallel",)),
        name="mix",
    )(y_m, y_a, z, z, w_m, w_a, w_o, x)


def _first_argmax(vals, vmax):
    idx = jnp.full(vals[0].shape, len(vals), jnp.int32)
    for j in reversed(range(len(vals))):
        idx = jnp.where(vals[j] == vmax, j, idx)
    return idx


def _pack_bf16_pairs(h):
    n = h.shape[1] // 2
    hb = h.astype(BF16).astype(F32)
    lo = lax.bitcast_convert_type(hb[:, :n], jnp.uint32) >> 16
    hi = lax.bitcast_convert_type(hb[:, n:], jnp.uint32) & jnp.uint32(0xFFFF0000)
    return lo | hi


def _unpack_bf16_pairs(p):
    lo = lax.bitcast_convert_type(p << 16, F32)
    hi = lax.bitcast_convert_type(p & jnp.uint32(0xFFFF0000), F32)
    return jnp.concatenate([lo, hi], axis=1).astype(BF16)


def _router_kernel(x_ref, g_ref, wr_ref, br_ref, idx_ref, wts_ref, cnt_ref, hp_ref,
                   tri_scr, carry_scr):
    tm = x_ref.shape[0]

    @pl.when(pl.program_id(0) == 0)
    def _():
        r = lax.broadcasted_iota(jnp.int32, (tm, tm), 0)
        c = lax.broadcasted_iota(jnp.int32, (tm, tm), 1)
        tri_scr[...] = jnp.where(r < c, 1.0, 0.0).astype(BF16)
        carry_scr[...] = jnp.zeros_like(carry_scr)

    h = _rms(x_ref[...], g_ref[...])
    hp_ref[...] = _pack_bf16_pairs(h)
    logits = lax.dot_general(wr_ref[...], h, (((1,), (1,)), ((), ())),
                             preferred_element_type=F32,
                             precision=lax.Precision.HIGHEST) + br_ref[...]
    gl = [logits[j:j + 1, :] for j in range(N_GROUPS)]
    gmax = functools.reduce(jnp.maximum, gl)
    gi = _first_argmax(gl, gmax)
    gp = 1.0 / functools.reduce(jnp.add, [jnp.exp(v - gmax) for v in gl])
    eg = []
    for e in range(EXPERTS_PER_GROUP):
        v = logits[ROUTER_EXPERT_ROW0 + e:ROUTER_EXPERT_ROW0 + e + 1, :]
        for grp in range(1, N_GROUPS):
            r0 = ROUTER_EXPERT_ROW0 + grp * EXPERTS_PER_GROUP + e
            v = jnp.where(gi == grp, logits[r0:r0 + 1, :], v)
        eg.append(v)
    v1 = functools.reduce(jnp.maximum, eg)
    i1 = _first_argmax(eg, v1)
    eg2 = [jnp.where(i1 == e, -jnp.inf, eg[e]) for e in range(EXPERTS_PER_GROUP)]
    v2 = functools.reduce(jnp.maximum, eg2)
    i2 = _first_argmax(eg2, v2)
    e2 = jnp.exp(v2 - v1)
    w1 = gp / (1.0 + e2)
    w2 = gp * e2 / (1.0 + e2)
    eid1 = gi * EXPERTS_PER_GROUP + i1
    eid2 = gi * EXPERTS_PER_GROUP + i2

    erow = lax.broadcasted_iota(jnp.int32, (N_EXPERTS, tm), 0)
    oh1 = erow == eid1
    oh2 = erow == eid2
    oh = jnp.where(oh1 | oh2, 1.0, 0.0)
    before = jnp.dot(oh.astype(BF16), tri_scr[...], preferred_element_type=F32) + carry_scr[...]
    rank1 = jnp.sum(jnp.where(oh1, before, 0.0), axis=0, keepdims=True)
    rank2 = jnp.sum(jnp.where(oh2, before, 0.0), axis=0, keepdims=True)
    carry_scr[...] = carry_scr[...] + jnp.sum(oh, axis=1, keepdims=True)

    idx_ref[...] = jnp.zeros_like(idx_ref)
    idx_ref[0:1, :] = eid1
    idx_ref[1:2, :] = eid2
    idx_ref[2:3, :] = rank1.astype(jnp.int32)
    idx_ref[3:4, :] = rank2.astype(jnp.int32)
    wts_ref[...] = jnp.zeros_like(wts_ref)
    wts_ref[0:1, :] = w1
    wts_ref[1:2, :] = w2
    cnt_ref[...] = jnp.broadcast_to(carry_scr[...], cnt_ref.shape).astype(jnp.int32)


def _router(x, g, wr, br, tm):
    T, D = x.shape
    tm = min(tm, T)
    return pl.pallas_call(
        _router_kernel,
        grid=(T // tm,),
        in_specs=[pl.BlockSpec((tm, D), lambda i: (i, 0)),
                  pl.BlockSpec((1, D), lambda i: (0, 0)),
                  pl.BlockSpec((ROUTER_ROWS, D), lambda i: (0, 0)),
                  pl.BlockSpec((ROUTER_ROWS, 1), lambda i: (0, 0))],
        out_specs=[pl.BlockSpec((8, tm), lambda i: (0, i)),
                   pl.BlockSpec((8, tm), lambda i: (0, i)),
                   pl.BlockSpec((N_EXPERTS, 128), lambda i: (0, 0)),
                   pl.BlockSpec((tm, D // 2), lambda i: (i, 0))],
        out_shape=[jax.ShapeDtypeStruct((8, T), jnp.int32),
                   jax.ShapeDtypeStruct((8, T), F32),
                   jax.ShapeDtypeStruct((N_EXPERTS, 128), jnp.int32),
                   jax.ShapeDtypeStruct((T, D // 2), jnp.uint32)],
        scratch_shapes=[pltpu.VMEM((tm, tm), BF16), pltpu.VMEM((N_EXPERTS, 1), F32)],
        compiler_params=_params(("arbitrary",)),
        name="router",
    )(x, g, wr, br)


def _slots_kernel(idx_ref, cnt_ref, dest_ref, blk_ref):
    shift = MOE_BLOCK.bit_length() - 1
    cnt = cnt_ref[...]
    padded = ((cnt + (MOE_BLOCK - 1)) >> shift) << shift
    eid1, eid2 = idx_ref[0:1, :], idx_ref[1:2, :]
    d1, d2 = idx_ref[2:3, :], idx_ref[3:4, :]
    blk_start = lax.broadcasted_iota(jnp.int32, (1, blk_ref.shape[1]), 1) * MOE_BLOCK
    blk_e = jnp.zeros_like(blk_start)
    acc = jnp.zeros((1, 1), jnp.int32)
    for e in range(N_EXPERTS):
        d1 = d1 + jnp.where(eid1 == e, acc, 0)
        d2 = d2 + jnp.where(eid2 == e, acc, 0)
        acc = acc + padded[e:e + 1, 0:1]
        blk_e = blk_e + jnp.where(acc <= blk_start, 1, 0)
    dest_ref[...] = jnp.zeros_like(dest_ref)
    dest_ref[0:1, :] = d1
    dest_ref[1:2, :] = d2
    blk_ref[...] = jnp.zeros_like(blk_ref)
    blk_ref[0:1, :] = jnp.minimum(blk_e, N_EXPERTS - 1)


def _slots(idx, cnt, n_blk, tm):
    T = idx.shape[1]
    tm = min(tm, T)
    lanes = -(-n_blk // 128) * 128
    return pl.pallas_call(
        _slots_kernel,
        grid=(T // tm,),
        in_specs=[pl.BlockSpec((8, tm), lambda i: (0, i)),
                  pl.BlockSpec((N_EXPERTS, 128), lambda i: (0, 0))],
        out_specs=[pl.BlockSpec((8, tm), lambda i: (0, i)),
                   pl.BlockSpec((8, lanes), lambda i: (0, 0))],
        out_shape=[jax.ShapeDtypeStruct((8, T), jnp.int32),
                   jax.ShapeDtypeStruct((8, lanes), jnp.int32)],
        compiler_params=_params(("arbitrary",)),
        name="slots",
    )(idx, cnt)


def _invert_kernel(dest_ref, slot_ref, *, n_tok, slot_chunk, tok_chunk):
    phase, step = pl.program_id(0), pl.program_id(1)

    @pl.when(phase == 0)
    def _():
        def zero(s, carry):
            slot_ref[step * slot_chunk + s] = 0
            return carry
        lax.fori_loop(0, slot_chunk, zero, 0, unroll=8)

    @pl.when(phase == 1)
    def _():
        def put(r, carry):
            t = step * tok_chunk + r
            slot_ref[dest_ref[t]] = t
            slot_ref[dest_ref[n_tok + t]] = t
            return carry
        lax.fori_loop(0, tok_chunk, put, 0, unroll=8)


def _invert(dest_flat, n_slots, n_tok, steps):
    while n_slots % steps or n_tok % steps:
        steps //= 2
    return pl.pallas_call(
        functools.partial(_invert_kernel, n_tok=n_tok, slot_chunk=n_slots // steps,
                          tok_chunk=n_tok // steps),
        grid=(2, steps),
        in_specs=[pl.BlockSpec(memory_space=pltpu.SMEM)],
        out_specs=pl.BlockSpec(memory_space=pltpu.SMEM),
        out_shape=jax.ShapeDtypeStruct((n_slots,), jnp.int32),
        compiler_params=_params(("arbitrary", "arbitrary")),
        name="invert",
    )(dest_flat)


def _expert_kernel(be_ref, tok_ref, hp_hbm, wg_ref, wu_ref, wd_ref, ys_ref,
                   xbuf, wg_scr, wu_scr, wd_scr, sem):
    i = pl.program_id(0)
    last = pl.num_programs(0) - 1

    def row_copy(blk, buf, r):
        t = tok_ref[blk * MOE_BLOCK + r]
        return pltpu.make_async_copy(hp_hbm.at[pl.ds(t, 1)], xbuf.at[buf, pl.ds(r, 1)], sem.at[buf])

    def all_rows(fn):
        def body(r, carry):
            fn(r)
            return carry
        lax.fori_loop(0, MOE_BLOCK, body, 0, unroll=8)

    @pl.when(i == 0)
    def _():
        all_rows(lambda r: row_copy(0, 0, r).start())

    all_rows(lambda r: row_copy(i, i % 2, r).wait())

    @pl.when((i == 0) | (be_ref[i] != be_ref[jnp.maximum(i - 1, 0)]))
    def _():
        def cast(src, dst):
            step = 256
            def body(c, carry):
                r = pl.ds(pl.multiple_of(c * step, step), step)
                dst[r, :] = src[0, 0, r, :].astype(BF16)
                return carry
            lax.fori_loop(0, dst.shape[0] // step, body, 0)
        cast(wg_ref, wg_scr)
        cast(wu_ref, wu_scr)
        cast(wd_ref, wd_scr)

    nxt = jnp.minimum(i + 1, last)
    for r in range(MOE_BLOCK):
        row_copy(nxt, (i + 1) % 2, r).start(priority=r % 2)
    h = _unpack_bf16_pairs(xbuf[i % 2])
    a = jnp.dot(h, wg_scr[...], preferred_element_type=F32)
    u = jnp.dot(h, wu_scr[...], preferred_element_type=F32)
    act = (a * jax.nn.sigmoid(a) * u).astype(BF16)
    ys_ref[...] = jnp.dot(act, wd_scr[...], preferred_element_type=F32)

    @pl.when(i == last)
    def _():
        all_rows(lambda r: row_copy(nxt, (i + 1) % 2, r).wait())


def _experts(blk_e, slot_tok, hp, w_g, w_u, w_d, layer):
    _, _, D, FF = w_g.shape
    n_blk = blk_e.shape[0]
    return pl.pallas_call(
        _expert_kernel,
        grid_spec=pltpu.PrefetchScalarGridSpec(
            num_scalar_prefetch=2,
            grid=(n_blk,),
            in_specs=[pl.BlockSpec(memory_space=pl.ANY),
                      pl.BlockSpec((1, 1, D, FF), lambda i, be, tok: (layer, be[i], 0, 0)),
                      pl.BlockSpec((1, 1, D, FF), lambda i, be, tok: (layer, be[i], 0, 0)),
                      pl.BlockSpec((1, 1, FF, D), lambda i, be, tok: (layer, be[i], 0, 0))],
            out_specs=pl.BlockSpec((MOE_BLOCK, D), lambda i, be, tok: (i, 0)),
            scratch_shapes=[pltpu.VMEM((2, MOE_BLOCK, D // 2), jnp.uint32),
                            pltpu.VMEM((D, FF), BF16), pltpu.VMEM((D, FF), BF16),
                            pltpu.VMEM((FF, D), BF16),
                            pltpu.SemaphoreType.DMA((2,))]),
        out_shape=jax.ShapeDtypeStruct((n_blk * MOE_BLOCK, D), F32),
        compiler_params=_params(("arbitrary",)),
        name="experts",
    )(blk_e, slot_tok, hp, w_g, w_u, w_d)


def _combine_kernel(dest_ref, x_ref, w_ref, ys_hbm, o_ref, ybuf, sem, *, n_tok):
    tm = x_ref.shape[0]
    i = pl.program_id(0)

    def row_copy(step, r, k):
        d = dest_ref[k * n_tok + step * tm + r]
        buf = step % 2
        return pltpu.make_async_copy(ys_hbm.at[pl.ds(d, 1)], ybuf.at[buf, k, pl.ds(r, 1)],
                                     sem.at[buf])

    def start_step(step):
        def body(r, carry):
            row_copy(step, r, 0).start(priority=0)
            row_copy(step, r, 1).start(priority=1)
            return carry
        lax.fori_loop(0, tm, body, 0, unroll=8)

    @pl.when(i == 0)
    def _():
        start_step(0)

    @pl.when(i + 1 < pl.num_programs(0))
    def _():
        start_step(i + 1)

    def wait(r, carry):
        row_copy(i, r, 0).wait()
        row_copy(i, r, 1).wait()
        return carry

    lax.fori_loop(0, tm, wait, 0, unroll=8)
    w = w_ref[...]
    o_ref[...] = x_ref[...] + w[:, 0:1] * ybuf[i % 2, 0] + w[:, 1:2] * ybuf[i % 2, 1]


def _combine(dest_flat, x, w_cols, ys, tm):
    T, D = x.shape
    tm = min(tm, T)
    return pl.pallas_call(
        functools.partial(_combine_kernel, n_tok=T),
        grid_spec=pltpu.PrefetchScalarGridSpec(
            num_scalar_prefetch=1,
            grid=(T // tm,),
            in_specs=[pl.BlockSpec((tm, D), lambda i, d: (i, 0)),
                      pl.BlockSpec((tm, 2), lambda i, d: (i, 0)),
                      pl.BlockSpec(memory_space=pl.ANY)],
            out_specs=pl.BlockSpec((tm, D), lambda i, d: (i, 0)),
            scratch_shapes=[pltpu.VMEM((2, 2, tm, D), F32), pltpu.SemaphoreType.DMA((2,))]),
        out_shape=jax.ShapeDtypeStruct((T, D), F32),
        compiler_params=_params(("arbitrary",)),
        name="combine",
    )(dest_flat, x, w_cols, ys)


def _ple_kernel(x_ref, g_ref, p_ref, wg_ref, wp_ref, gn_ref, o_ref, *rest, tn, last):
    h_scr = rest[-1]
    rows, D = x_ref.shape
    _norm_rows_into(x_ref, g_ref, h_scr)
    pb = p_ref[...].astype(BF16)
    for j in range(D // tn):
        cols = slice(j * tn, (j + 1) * tn)
        gate = jax.nn.sigmoid(jnp.dot(h_scr[...], wg_ref[:, cols], preferred_element_type=F32))
        proj = jnp.dot(pb, wp_ref[:, cols], preferred_element_type=F32)
        o_ref[:, cols] = x_ref[:, cols] + gate * proj
    if last:
        step = min(NORM_ROWS, rows)

        def body(c, carry):
            r = pl.ds(pl.multiple_of(c * step, step), step)
            o_ref[r, :] = _rms(o_ref[r, :], gn_ref[...])
            return carry

        lax.fori_loop(0, rows // step, body, 0)
    else:
        _norm_rows_into(o_ref, gn_ref, rest[0])


def _ple(x, g, p, w_g, w_p, g_next, tm, tn, last):
    T, D = x.shape
    PD = p.shape[1]
    tm, tn = min(tm, T), min(tn, D)
    row_tile = pl.BlockSpec((tm, D), lambda i: (i, 0))
    gain = pl.BlockSpec((1, D), lambda i: (0, 0))
    return pl.pallas_call(
        functools.partial(_ple_kernel, tn=tn, last=last),
        grid=(T // tm,),
        in_specs=[row_tile, gain, pl.BlockSpec((tm, PD), lambda i: (i, 0)),
                  pl.BlockSpec((D, D), lambda i: (0, 0)), pl.BlockSpec((PD, D), lambda i: (0, 0)), gain],
        out_specs=[row_tile] if last else [row_tile, row_tile],
        out_shape=[jax.ShapeDtypeStruct((T, D), F32)] + ([] if last else [jax.ShapeDtypeStruct((T, D), BF16)]),
        scratch_shapes=[pltpu.VMEM((tm, D), BF16)],
        compiler_params=_params(("parallel",)),
        name="ple",
    )(x, g, p, w_g, w_p, g_next)


def _rope_tables(S):
    pos = np.arange(S)
    inv_freq = ROPE_THETA ** (-np.arange(ROPE_PAIRS, dtype=np.float32) / ROPE_PAIRS)
    ang_row = jnp.asarray((pos // GRID_W).astype(np.float32)[:, None] * inv_freq[None, :])
    ang_col = jnp.asarray((pos % GRID_W).astype(np.float32)[:, None] * inv_freq[None, :])
    zero = jnp.zeros_like(ang_row)
    cos_t = jnp.concatenate([jnp.cos(ang_row)] * 2 + [jnp.cos(ang_col)] * 2, axis=1)
    sin_lo = jnp.concatenate([-jnp.sin(ang_row), zero, -jnp.sin(ang_col), zero], axis=1)
    sin_hi = jnp.concatenate([zero, jnp.sin(ang_row), zero, jnp.sin(ang_col)], axis=1)
    return cos_t, sin_lo, sin_hi


def kernel(x, p, norm1_g, w_in, mlstm_gate_b, mlstm_norm_g, q_norm_g, k_norm_g, w_branch_m, w_branch_a, w_out, norm2_g, w_router_group, b_router_group, w_router_expert, b_router_expert, w_exp_gate, w_exp_up, w_exp_down, ple_norm_g, w_ple_gate, w_ple_proj, final_norm_g):
    B, S, D = x.shape
    depth = w_in.shape[0]
    T = B * S
    H = MLSTM_HEADS
    scale = HEAD_DIM ** -0.5
    tables = _rope_tables(S)
    row = lambda v: v.reshape(1, -1).astype(F32)

    o_qm, o_gates = 0, 4 * MLSTM_W
    o_qa = o_gates + 4 * H
    o_gm = o_qa + ATTN_Q_W + 2 * ATTN_KV_W
    o_ga = o_gm + D
    z_m = 2 * D
    z_qa = z_m + 4 * MLSTM_W

    xf = x.reshape(T, D)
    h = _norm_cast(xf, row(norm1_g[0]), 1024)
    for i in range(depth):
        wi = w_in[i]
        w_main = jnp.concatenate([wi[:, o_gm:], wi[:, o_qm:o_qm + MLSTM_W] * scale,
                                  wi[:, MLSTM_W:o_gates], wi[:, o_qa:o_gm]], axis=1).astype(BF16)
        lane_pad = ((0, 0), (0, GATE_LANES - 2 * H))
        w_gates = jnp.concatenate([jnp.pad(wi[:, o_gates:o_gates + 2 * H], lane_pad),
                                   jnp.pad(wi[:, o_gates + 2 * H:o_qa], lane_pad)], axis=1).astype(BF16)
        gb = mlstm_gate_b[i].astype(F32).reshape(1, 4 * H)
        gate_bias = jnp.concatenate([jnp.pad(gb[:, :2 * H], lane_pad),
                                     jnp.pad(gb[:, 2 * H:], lane_pad)], axis=1)
        z = _inproj(h, w_main, 4, 512)
        gates = _gates_proj(h, w_gates, 1024)
        z3 = z.reshape(B, S, -1)
        y_m = _mlstm(z3, z_m, gates.reshape(B, S, -1), gate_bias, row(mlstm_norm_g[i]))
        y_a = _attention(z3, z_qa // HEAD_DIM, tables, row(q_norm_g[i]) * (scale * np.log2(np.e)),
                         row(k_norm_g[i]), 256)
        xf = _mix(y_m.reshape(T, MLSTM_W), y_a.reshape(T, ATTN_Q_W), z, 0, 1,
                  w_branch_m[i].astype(BF16), w_branch_a[i].astype(BF16), w_out[i].astype(BF16),
                  xf, 256, 512)

        wr = jnp.zeros((ROUTER_ROWS, D), F32)
        wr = wr.at[:N_GROUPS].set(w_router_group[i].T)
        wr = wr.at[ROUTER_EXPERT_ROW0:ROUTER_EXPERT_ROW0 + N_EXPERTS].set(w_router_expert[i].T)
        br = jnp.zeros((ROUTER_ROWS, 1), F32)
        br = br.at[:N_GROUPS, 0].set(b_router_group[i])
        br = br.at[ROUTER_EXPERT_ROW0:ROUTER_EXPERT_ROW0 + N_EXPERTS, 0].set(b_router_expert[i])
        idx, wts, cnt, hp = _router(xf, row(norm2_g[i]), wr, br, 512)
        n_blk = -(-2 * T // MOE_BLOCK) + N_EXPERTS
        dest8, blk = _slots(idx, cnt, n_blk, 4096)
        dest = dest8[0:2].reshape(-1)
        slot_tok = _invert(dest, n_blk * MOE_BLOCK, T, 64)
        ys = _experts(blk[0, :n_blk], slot_tok, hp, w_exp_gate, w_exp_up, w_exp_down, i)
        xf = _combine(dest, xf, wts[0:2].T, ys, 256)

        last = i == depth - 1
        out = _ple(xf, row(ple_norm_g[i]), p[i].reshape(T, -1), w_ple_gate[i].astype(BF16),
                   w_ple_proj[i].astype(BF16), row(final_norm_g if last else norm1_g[i + 1]),
                   512, 512, last)
        xf = out[0]
        if not last:
            h = out[1]

    return xf.reshape(B, S, D)
```

```python
import functools

import jax
import jax.numpy as jnp
import numpy as np
from jax import lax
from jax.experimental import pallas as pl
from jax.experimental.pallas import tpu as pltpu

F32 = jnp.float32
BF16 = jnp.bfloat16

GRID_W = 64
HEAD_DIM = 128
MLSTM_HEADS = 8
MLSTM_W = MLSTM_HEADS * HEAD_DIM
MLSTM_CHUNK = 128
MLSTM_HEADS_PER_STEP = 4
MLSTM_UNROLL = 4
MLSTM_DEN_ROWS = 16
ATTN_HEADS = 8
ATTN_KV_HEADS = 2
ATTN_GROUP = ATTN_HEADS // ATTN_KV_HEADS
ATTN_Q_W = ATTN_HEADS * HEAD_DIM
ATTN_KV_W = ATTN_KV_HEADS * HEAD_DIM
ROPE_THETA = 10000.0
ROPE_PAIRS = HEAD_DIM // 4
N_GROUPS = 4
EXPERTS_PER_GROUP = 8
N_EXPERTS = N_GROUPS * EXPERTS_PER_GROUP
MOE_BLOCK = 256
EPS = 1e-6
GATE_LANES = 128
ROUTER_ROWS = 128
ROUTER_EXPERT_ROW0 = 8
VMEM_LIMIT_BYTES = 52 * 1024 * 1024
NORM_ROWS = 128


def _params(semantics, **kw):
    return pltpu.CompilerParams(dimension_semantics=semantics,
                                vmem_limit_bytes=VMEM_LIMIT_BYTES, **kw)


def _rms(x, g):
    ms = jnp.mean(x * x, axis=-1, keepdims=True)
    return (x * lax.rsqrt(ms + EPS)) * g


def _norm_rows_into(x_ref, g_ref, h_scr):
    rows = x_ref.shape[0]
    step = min(NORM_ROWS, rows)

    def body(c, carry):
        r = pl.ds(pl.multiple_of(c * step, step), step)
        h_scr[r, :] = _rms(x_ref[r, :], g_ref[...]).astype(h_scr.dtype)
        return carry

    lax.fori_loop(0, rows // step, body, 0)


def _norm_cast_kernel(x_ref, g_ref, h_ref):
    h_ref[...] = _rms(x_ref[...], g_ref[...]).astype(h_ref.dtype)


def _norm_cast(x, g, tm):
    T, D = x.shape
    tm = min(tm, T)
    return pl.pallas_call(
        _norm_cast_kernel,
        grid=(T // tm,),
        in_specs=[pl.BlockSpec((tm, D), lambda i: (i, 0)), pl.BlockSpec((1, D), lambda i: (0, 0))],
        out_specs=pl.BlockSpec((tm, D), lambda i: (i, 0)),
        out_shape=jax.ShapeDtypeStruct((T, D), BF16),
        compiler_params=_params(("parallel",)),
        name="norm_cast",
    )(x, g)


def _matmul_kernel(h_ref, w_ref, o_ref):
    o_ref[...] = jnp.dot(h_ref[...], w_ref[...], preferred_element_type=F32).astype(o_ref.dtype)


def _inproj(h, w, col_groups, tm):
    T, D = h.shape
    N = w.shape[1]
    tm = min(tm, T)
    tn = N // col_groups
    assert tn * col_groups == N and tn % 128 == 0
    return pl.pallas_call(
        _matmul_kernel,
        grid=(col_groups, T // tm),
        in_specs=[pl.BlockSpec((tm, D), lambda j, i: (i, 0)),
                  pl.BlockSpec((D, tn), lambda j, i: (0, j))],
        out_specs=pl.BlockSpec((tm, tn), lambda j, i: (i, j)),
        out_shape=jax.ShapeDtypeStruct((T, N), BF16),
        compiler_params=_params(("arbitrary", "arbitrary")),
        name="inproj",
    )(h, w)


def _gates_proj(h, wg, tm):
    T, D = h.shape
    N = wg.shape[1]
    tm = min(tm, T)
    return pl.pallas_call(
        _matmul_kernel,
        grid=(T // tm,),
        in_specs=[pl.BlockSpec((tm, D), lambda i: (i, 0)), pl.BlockSpec((D, N), lambda i: (0, 0))],
        out_specs=pl.BlockSpec((tm, N), lambda i: (i, 0)),
        out_shape=jax.ShapeDtypeStruct((T, N), F32),
        compiler_params=_params(("parallel",)),
        name="gates_proj",
    )(h, wg)


def _log_sigmoid(x):
    return jnp.minimum(x, 0.0) - jnp.log1p(jnp.exp(-jnp.abs(x)))


def _mlstm_kernel(q_ref, k_ref, v_ref, o_ref, gi_ref, gf_ref, bi_ref, bf_ref, ng_ref, y_ref,
                  up_scr, bt_scr, ut_scr, cm_scr, qt_scr, vxt_scr, a_scr, ht_scr, c_scr):
    L, d = MLSTM_CHUNK, HEAD_DIM
    S = q_ref.shape[1]
    nc = S // L
    heads = q_ref.shape[2] // d
    head0 = pl.program_id(1) * heads
    srow = lax.broadcasted_iota(jnp.int32, (L, L), 0)
    tcol = lax.broadcasted_iota(jnp.int32, (L, L), 1)
    mask2 = jnp.concatenate([srow <= tcol, srow >= tcol], axis=1)
    lasts = (L - 1, 0)
    tri2 = jnp.concatenate([jnp.where(tcol <= srow, 1.0, 0.0),
                            jnp.where(tcol >= srow, 1.0, 0.0)], axis=0).astype(BF16)
    fw_lane = lax.broadcasted_iota(jnp.int32, (L, GATE_LANES), 1) < MLSTM_HEADS
    ones_t = jnp.ones((MLSTM_DEN_ROWS, L), BF16)
    ch_row = lax.broadcasted_iota(jnp.int32, (GATE_LANES, 2 * L), 0)
    ch_fw = lax.broadcasted_iota(jnp.int32, (GATE_LANES, 2 * L), 1) < L

    def split3(x):
        p1 = x.astype(BF16)
        r1 = x - p1.astype(F32)
        p2 = r1.astype(BF16)
        p3 = (r1 - p2.astype(F32)).astype(BF16)
        return p1, p2, p3

    def chunk(c):
        return pl.ds(pl.multiple_of(c * L, L), L)

    def gates_pass(c, carry):
        sl = chunk(c)
        ls = _log_sigmoid(gf_ref[0, sl, :] + bf_ref[...])
        cs = sum(jnp.dot(tri2, part, preferred_element_type=F32) for part in split3(ls))
        bn = jnp.where(fw_lane, cs[:L], cs[L:])
        un = gi_ref[0, sl, :] + bi_ref[...] - bn
        for i, part in enumerate(split3(un)):
            up_scr[i, sl, :] = part
        bt_scr[c] = bn.T[:2 * MLSTM_HEADS]
        ut_scr[c] = un.T[:2 * MLSTM_HEADS]
        return carry

    def head_passes(hh):
        hl = slice(hh * d, (hh + 1) * d)
        head = head0 + hh
        sel2 = jnp.where(ch_row == jnp.where(ch_fw, head, MLSTM_HEADS + head), 1.0, 0.0).astype(BF16)

        def intra(c, carry):
            sl = chunk(c)
            q_t = q_ref[0, sl, hl].astype(F32).T.astype(BF16)
            vx_t = jnp.concatenate([v_ref[0, sl, hl].astype(F32).T.astype(BF16), ones_t], axis=0)
            qt_scr[c] = q_t
            vxt_scr[c] = vx_t
            kq = jnp.dot(k_ref[0, sl, hl], q_t, preferred_element_type=F32)
            ucol = sum(jnp.dot(up_scr[i, sl, :], sel2, preferred_element_type=F32)
                       for i in range(3))
            cmu = jnp.max(jnp.where(mask2, ucol, -jnp.inf), axis=0, keepdims=True)
            e0 = jnp.where(mask2, jnp.exp(ucol - cmu), 0.0)
            s0 = (jnp.concatenate([kq, kq], axis=1) * e0).astype(BF16)
            a_scr[c] = jnp.dot(vx_t, s0, preferred_element_type=F32)
            cm_scr[c] = cmu
            return carry

        def recurrence(j, ms):
            out = []
            for dd in range(2):
                c = j if dd == 0 else nc - 1 - j
                tl = slice(dd * L, (dd + 1) * L)
                m = ms[dd]
                row = pl.ds(dd * MLSTM_HEADS + head, 1)
                urow = ut_scr[c, row, :]
                brow = bt_scr[c, row, :]
                cmu = cm_scr[c, :, tl]
                b_last = brow[:, lasts[dd]:lasts[dd] + 1]
                u_max = cmu[:, lasts[dd]:lasts[dd] + 1]
                mm = jnp.maximum(cmu, m)
                cx = c_scr[dd]
                numx = (a_scr[c, :, tl] * jnp.exp(cmu - mm) + jnp.exp(m - mm)
                        * jnp.dot(cx.astype(BF16), qt_scr[c], preferred_element_type=F32))
                den = jnp.maximum(jnp.abs(numx[d:d + 1]), jnp.exp(-(brow + mm)))
                ht_scr[dd, c] = numx[:d] * (1.0 / den)
                m_new = b_last + jnp.maximum(m, u_max)
                wk = jnp.exp(b_last + urow - m_new)
                vw = (vxt_scr[c].astype(F32) * wk).astype(BF16)
                c_scr[dd] = (jnp.exp(b_last + m - m_new) * cx
                             + jnp.dot(vw, k_ref[0, chunk(c), hl], preferred_element_type=F32))
                out.append(m_new)
            return tuple(out)

        def finish(c, carry):
            sl = chunk(c)
            hn = _rms((ht_scr[0, c] + ht_scr[1, c]).T, ng_ref[:, hl])
            y_ref[0, sl, hl] = (hn * jax.nn.sigmoid(o_ref[0, sl, hl].astype(F32))).astype(y_ref.dtype)
            return carry

        c_scr[...] = jnp.zeros_like(c_scr)
        lax.fori_loop(0, nc, intra, 0, unroll=2 * MLSTM_UNROLL)
        lax.fori_loop(0, nc, recurrence, (jnp.zeros((1, 1), F32),) * 2, unroll=MLSTM_UNROLL)
        lax.fori_loop(0, nc, finish, 0, unroll=MLSTM_UNROLL)

    lax.fori_loop(0, nc, gates_pass, 0, unroll=MLSTM_UNROLL)
    for hh in range(heads):
        head_passes(hh)


def _mlstm(z3, col0, gates3, gate_bias, norm_g):
    B, S, _ = z3.shape
    H, hps = MLSTM_HEADS, MLSTM_HEADS_PER_STEP
    L, d = MLSTM_CHUNK, HEAD_DIM
    nc = S // L
    w = hps * d
    rows = d + MLSTM_DEN_ROWS
    blk0 = col0 // w
    assert blk0 * w == col0
    col = lambda part: pl.BlockSpec((1, S, w), lambda b, h: (b, 0, blk0 + part * (H // hps) + h))
    gate = lambda part: pl.BlockSpec((1, S, GATE_LANES), lambda b, h: (b, 0, part))
    bias = lambda part: pl.BlockSpec((1, GATE_LANES), lambda b, h: (0, part))
    return pl.pallas_call(
        _mlstm_kernel,
        grid=(B, H // hps),
        in_specs=[col(0), col(1), col(2), col(3), gate(0), gate(1), bias(0), bias(1),
                  pl.BlockSpec((1, w), lambda b, h: (0, h))],
        out_specs=pl.BlockSpec((1, S, w), lambda b, h: (b, 0, h)),
        out_shape=jax.ShapeDtypeStruct((B, S, MLSTM_W), BF16),
        scratch_shapes=[pltpu.VMEM((3, S, GATE_LANES), BF16),
                        pltpu.VMEM((nc, 2 * H, L), F32),
                        pltpu.VMEM((nc, 2 * H, L), F32),
                        pltpu.VMEM((nc, 1, 2 * L), F32),
                        pltpu.VMEM((nc, d, L), BF16),
                        pltpu.VMEM((nc, rows, L), BF16),
                        pltpu.VMEM((nc, rows, 2 * L), F32),
                        pltpu.VMEM((2, nc, d, L), F32),
                        pltpu.VMEM((2, rows, d), F32)],
        compiler_params=_params(("parallel", "arbitrary")),
        name="mlstm",
    )(z3, z3, z3, z3, gates3, gates3, gate_bias, gate_bias, norm_g)


def _rope(t, c, s_lo, s_hi):
    quarter = HEAD_DIM // 4
    return (t * c + pltpu.roll(t, HEAD_DIM - quarter, 1) * s_lo
            + pltpu.roll(t, quarter, 1) * s_hi)


def _attn_kernel(q_ref, k_ref, v_ref, cq_ref, sloq_ref, shiq_ref, ck_ref, slok_ref, shik_ref,
                 qg_ref, kg_ref, o_ref, k_scr):
    S = k_ref.shape[1]
    step = min(256, S)

    @pl.when(pl.program_id(2) == 0)
    def _():
        def body(c, carry):
            r = pl.ds(pl.multiple_of(c * step, step), step)
            kn = _rms(k_ref[0, r, :].astype(F32), kg_ref[...])
            k_scr[r, :] = _rope(kn, ck_ref[r, :], slok_ref[r, :], shik_ref[r, :]).astype(BF16)
            return carry

        lax.fori_loop(0, S // step, body, 0)

    for g in range(ATTN_GROUP):
        hl = slice(g * HEAD_DIM, (g + 1) * HEAD_DIM)
        qn = _rms(q_ref[0, :, hl].astype(F32), qg_ref[...])
        qr = _rope(qn, cq_ref[...], sloq_ref[...], shiq_ref[...]).astype(BF16)
        s = lax.dot_general(qr, k_scr[...], (((1,), (1,)), ((), ())), preferred_element_type=F32)
        p = jnp.exp2(s - jnp.max(s, axis=1, keepdims=True))
        l = jnp.sum(p, axis=1, keepdims=True)
        o = jnp.dot(p.astype(BF16), v_ref[0], preferred_element_type=F32)
        o_ref[0, :, hl] = (o / l).astype(o_ref.dtype)


def _attention(z3, q_blk0, tables, q_g, k_g, tq):
    B, S, _ = z3.shape
    tq = min(tq, S)
    cos_t, slo_t, shi_t = tables
    gw = ATTN_GROUP * HEAD_DIM
    q_grp0 = q_blk0 // ATTN_GROUP
    assert q_grp0 * ATTN_GROUP == q_blk0
    k_blk0 = q_blk0 + ATTN_HEADS
    v_blk0 = k_blk0 + ATTN_KV_HEADS
    qtab = pl.BlockSpec((tq, HEAD_DIM), lambda b, kv, i: (i, 0))
    ktab = pl.BlockSpec((S, HEAD_DIM), lambda b, kv, i: (0, 0))
    gain = pl.BlockSpec((1, HEAD_DIM), lambda b, kv, i: (0, 0))
    return pl.pallas_call(
        _attn_kernel,
        grid=(B, ATTN_KV_HEADS, S // tq),
        in_specs=[pl.BlockSpec((1, tq, gw), lambda b, kv, i: (b, i, q_grp0 + kv)),
                  pl.BlockSpec((1, S, HEAD_DIM), lambda b, kv, i: (b, 0, k_blk0 + kv)),
                  pl.BlockSpec((1, S, HEAD_DIM), lambda b, kv, i: (b, 0, v_blk0 + kv)),
                  qtab, qtab, qtab, ktab, ktab, ktab, gain, gain],
        out_specs=pl.BlockSpec((1, tq, gw), lambda b, kv, i: (b, i, kv)),
        out_shape=jax.ShapeDtypeStruct((B, S, ATTN_Q_W), BF16),
        scratch_shapes=[pltpu.VMEM((S, HEAD_DIM), BF16)],
        compiler_params=_params(("parallel", "parallel", "arbitrary")),
        name="attention",
    )(z3, z3, z3, cos_t, slo_t, shi_t, cos_t, slo_t, shi_t, q_g, k_g)


def _mix_kernel(ym_ref, ya_ref, gm_ref, ga_ref, wm_ref, wa_ref, wo_ref, x_ref, o_ref, mg_scr, *, tn):
    D = x_ref.shape[1]
    for j in range(D // tn):
        cols = slice(j * tn, (j + 1) * tn)
        bm = jnp.dot(ym_ref[...], wm_ref[:, cols], preferred_element_type=F32)
        ba = jnp.dot(ya_ref[...], wa_ref[:, cols], preferred_element_type=F32)
        mg_scr[:, cols] = (jax.nn.sigmoid(gm_ref[:, cols].astype(F32)) * bm
                           + jax.nn.sigmoid(ga_ref[:, cols].astype(F32)) * ba).astype(mg_scr.dtype)
    for j in range(D // tn):
        cols = slice(j * tn, (j + 1) * tn)
        o_ref[:, cols] = x_ref[:, cols] + jnp.dot(mg_scr[...], wo_ref[:, cols],
                                                  preferred_element_type=F32)


def _mix(y_m, y_a, z, gm_blk, ga_blk, w_m, w_a, w_o, x, tm, tn):
    T, K = y_m.shape
    D = x.shape[1]
    tm, tn = min(tm, T), min(tn, D)
    return pl.pallas_call(
        functools.partial(_mix_kernel, tn=tn),
        grid=(T // tm,),
        in_specs=[pl.BlockSpec((tm, K), lambda i: (i, 0)),
                  pl.BlockSpec((tm, K), lambda i: (i, 0)),
                  pl.BlockSpec((tm, D), lambda i: (i, gm_blk)),
                  pl.BlockSpec((tm, D), lambda i: (i, ga_blk)),
                  pl.BlockSpec((K, D), lambda i: (0, 0)),
                  pl.BlockSpec((K, D), lambda i: (0, 0)),
                  pl.BlockSpec((D, D), lambda i: (0, 0)),
                  pl.BlockSpec((tm, D), lambda i: (i, 0))],
        out_specs=pl.BlockSpec((tm, D), lambda i: (i, 0)),
        out_shape=jax.ShapeDtypeStruct((T, D), F32),
        scratch_shapes=[pltpu.VMEM((tm, D), BF16)],
        compiler_params=_params(("parallel",)),
        name="mix",
    )(y_m, y_a, z, z, w_m, w_a, w_o, x)


def _first_argmax(vals, vmax):
    idx = jnp.full(vals[0].shape, len(vals), jnp.int32)
    for j in reversed(range(len(vals))):
        idx = jnp.where(vals[j] == vmax, j, idx)
    return idx


def _pack_bf16_pairs(h):
    n = h.shape[1] // 2
    hb = h.astype(BF16).astype(F32)
    lo = lax.bitcast_convert_type(hb[:, :n], jnp.uint32) >> 16
    hi = lax.bitcast_convert_type(hb[:, n:], jnp.uint32) & jnp.uint32(0xFFFF0000)
    return lo | hi


def _unpack_bf16_pairs(p):
    lo = lax.bitcast_convert_type(p << 16, F32)
    hi = lax.bitcast_convert_type(p & jnp.uint32(0xFFFF0000), F32)
    return jnp.concatenate([lo, hi], axis=1).astype(BF16)


def _router_kernel(x_ref, g_ref, wr_ref, br_ref, idx_ref, wts_ref, cnt_ref, hp_ref,
                   tri_scr, carry_scr):
    tm = x_ref.shape[0]

    @pl.when(pl.program_id(0) == 0)
    def _():
        r = lax.broadcasted_iota(jnp.int32, (tm, tm), 0)
        c = lax.broadcasted_iota(jnp.int32, (tm, tm), 1)
        tri_scr[...] = jnp.where(r < c, 1.0, 0.0).astype(BF16)
        carry_scr[...] = jnp.zeros_like(carry_scr)

    h = _rms(x_ref[...], g_ref[...])
    hp_ref[...] = _pack_bf16_pairs(h)
    logits = lax.dot_general(wr_ref[...], h, (((1,), (1,)), ((), ())),
                             preferred_element_type=F32,
                             precision=lax.Precision.HIGHEST) + br_ref[...]
    gl = [logits[j:j + 1, :] for j in range(N_GROUPS)]
    gmax = functools.reduce(jnp.maximum, gl)
    gi = _first_argmax(gl, gmax)
    gp = 1.0 / functools.reduce(jnp.add, [jnp.exp(v - gmax) for v in gl])
    eg = []
    for e in range(EXPERTS_PER_GROUP):
        v = logits[ROUTER_EXPERT_ROW0 + e:ROUTER_EXPERT_ROW0 + e + 1, :]
        for grp in range(1, N_GROUPS):
            r0 = ROUTER_EXPERT_ROW0 + grp * EXPERTS_PER_GROUP + e
            v = jnp.where(gi == grp, logits[r0:r0 + 1, :], v)
        eg.append(v)
    v1 = functools.reduce(jnp.maximum, eg)
    i1 = _first_argmax(eg, v1)
    eg2 = [jnp.where(i1 == e, -jnp.inf, eg[e]) for e in range(EXPERTS_PER_GROUP)]
    v2 = functools.reduce(jnp.maximum, eg2)
    i2 = _first_argmax(eg2, v2)
    e2 = jnp.exp(v2 - v1)
    w1 = gp / (1.0 + e2)
    w2 = gp * e2 / (1.0 + e2)
    eid1 = gi * EXPERTS_PER_GROUP + i1
    eid2 = gi * EXPERTS_PER_GROUP + i2

    erow = lax.broadcasted_iota(jnp.int32, (N_EXPERTS, tm), 0)
    oh1 = erow == eid1
    oh2 = erow == eid2
    oh = jnp.where(oh1 | oh2, 1.0, 0.0)
    before = jnp.dot(oh.astype(BF16), tri_scr[...], preferred_element_type=F32) + carry_scr[...]
    rank1 = jnp.sum(jnp.where(oh1, before, 0.0), axis=0, keepdims=True)
    rank2 = jnp.sum(jnp.where(oh2, before, 0.0), axis=0, keepdims=True)
    carry_scr[...] = carry_scr[...] + jnp.sum(oh, axis=1, keepdims=True)

    idx_ref[...] = jnp.zeros_like(idx_ref)
    idx_ref[0:1, :] = eid1
    idx_ref[1:2, :] = eid2
    idx_ref[2:3, :] = rank1.astype(jnp.int32)
    idx_ref[3:4, :] = rank2.astype(jnp.int32)
    wts_ref[...] = jnp.zeros_like(wts_ref)
    wts_ref[0:1, :] = w1
    wts_ref[1:2, :] = w2
    cnt_ref[...] = jnp.broadcast_to(carry_scr[...], cnt_ref.shape).astype(jnp.int32)


def _router(x, g, wr, br, tm):
    T, D = x.shape
    tm = min(tm, T)
    return pl.pallas_call(
        _router_kernel,
        grid=(T // tm,),
        in_specs=[pl.BlockSpec((tm, D), lambda i: (i, 0)),
                  pl.BlockSpec((1, D), lambda i: (0, 0)),
                  pl.BlockSpec((ROUTER_ROWS, D), lambda i: (0, 0)),
                  pl.BlockSpec((ROUTER_ROWS, 1), lambda i: (0, 0))],
        out_specs=[pl.BlockSpec((8, tm), lambda i: (0, i)),
                   pl.BlockSpec((8, tm), lambda i: (0, i)),
                   pl.BlockSpec((N_EXPERTS, 128), lambda i: (0, 0)),
                   pl.BlockSpec((tm, D // 2), lambda i: (i, 0))],
        out_shape=[jax.ShapeDtypeStruct((8, T), jnp.int32),
                   jax.ShapeDtypeStruct((8, T), F32),
                   jax.ShapeDtypeStruct((N_EXPERTS, 128), jnp.int32),
                   jax.ShapeDtypeStruct((T, D // 2), jnp.uint32)],
        scratch_shapes=[pltpu.VMEM((tm, tm), BF16), pltpu.VMEM((N_EXPERTS, 1), F32)],
        compiler_params=_params(("arbitrary",)),
        name="router",
    )(x, g, wr, br)


def _slots_kernel(idx_ref, cnt_ref, dest_ref, blk_ref):
    shift = MOE_BLOCK.bit_length() - 1
    cnt = cnt_ref[...]
    padded = ((cnt + (MOE_BLOCK - 1)) >> shift) << shift
    eid1, eid2 = idx_ref[0:1, :], idx_ref[1:2, :]
    d1, d2 = idx_ref[2:3, :], idx_ref[3:4, :]
    blk_start = lax.broadcasted_iota(jnp.int32, (1, blk_ref.shape[1]), 1) * MOE_BLOCK
    blk_e = jnp.zeros_like(blk_start)
    acc = jnp.zeros((1, 1), jnp.int32)
    for e in range(N_EXPERTS):
        d1 = d1 + jnp.where(eid1 == e, acc, 0)
        d2 = d2 + jnp.where(eid2 == e, acc, 0)
        acc = acc + padded[e:e + 1, 0:1]
        blk_e = blk_e + jnp.where(acc <= blk_start, 1, 0)
    dest_ref[...] = jnp.zeros_like(dest_ref)
    dest_ref[0:1, :] = d1
    dest_ref[1:2, :] = d2
    blk_ref[...] = jnp.zeros_like(blk_ref)
    blk_ref[0:1, :] = jnp.minimum(blk_e, N_EXPERTS - 1)


def _slots(idx, cnt, n_blk, tm):
    T = idx.shape[1]
    tm = min(tm, T)
    lanes = -(-n_blk // 128) * 128
    return pl.pallas_call(
        _slots_kernel,
        grid=(T // tm,),
        in_specs=[pl.BlockSpec((8, tm), lambda i: (0, i)),
                  pl.BlockSpec((N_EXPERTS, 128), lambda i: (0, 0))],
        out_specs=[pl.BlockSpec((8, tm), lambda i: (0, i)),
                   pl.BlockSpec((8, lanes), lambda i: (0, 0))],
        out_shape=[jax.ShapeDtypeStruct((8, T), jnp.int32),
                   jax.ShapeDtypeStruct((8, lanes), jnp.int32)],
        compiler_params=_params(("arbitrary",)),
        name="slots",
    )(idx, cnt)


def _dispatch_kernel(dest_ref, hp_ref, xs_in, xs_hbm, sem, *, n_tok):
    del xs_in
    tm = hp_ref.shape[0]
    base = pl.program_id(0) * tm

    def row_copy(r, k):
        d = dest_ref[k * n_tok + base + r]
        return pltpu.make_async_copy(hp_ref.at[pl.ds(r, 1)], xs_hbm.at[pl.ds(d, 1)], sem)

    def start(r, carry):
        row_copy(r, 0).start(priority=0)
        row_copy(r, 1).start(priority=1)
        return carry

    def wait(r, carry):
        row_copy(r, 0).wait()
        row_copy(r, 1).wait()
        return carry

    lax.fori_loop(0, tm, start, 0, unroll=8)
    lax.fori_loop(0, tm, wait, 0, unroll=8)


def _dispatch(dest_flat, hp, xs_zero, tm):
    T, W = hp.shape
    tm = min(tm, T)
    return pl.pallas_call(
        functools.partial(_dispatch_kernel, n_tok=T),
        grid_spec=pltpu.PrefetchScalarGridSpec(
            num_scalar_prefetch=1,
            grid=(T // tm,),
            in_specs=[pl.BlockSpec((tm, W), lambda i, d: (i, 0)),
                      pl.BlockSpec(memory_space=pl.ANY)],
            out_specs=pl.BlockSpec(memory_space=pl.ANY),
            scratch_shapes=[pltpu.SemaphoreType.DMA]),
        out_shape=jax.ShapeDtypeStruct(xs_zero.shape, xs_zero.dtype),
        input_output_aliases={2: 0},
        compiler_params=_params(("arbitrary",)),
        name="dispatch",
    )(dest_flat, hp, xs_zero)


def _expert_kernel(be_ref, xs_ref, wg_ref, wu_ref, wd_ref, ys_ref, wg_scr, wu_scr, wd_scr):
    i = pl.program_id(0)

    @pl.when((i == 0) | (be_ref[i] != be_ref[jnp.maximum(i - 1, 0)]))
    def _():
        def cast(src, dst):
            step = 256
            def body(c, carry):
                r = pl.ds(pl.multiple_of(c * step, step), step)
                dst[r, :] = src[0, 0, r, :].astype(BF16)
                return carry
            lax.fori_loop(0, dst.shape[0] // step, body, 0)
        cast(wg_ref, wg_scr)
        cast(wu_ref, wu_scr)
        cast(wd_ref, wd_scr)

    h = _unpack_bf16_pairs(xs_ref[...])
    a = jnp.dot(h, wg_scr[...], preferred_element_type=F32)
    u = jnp.dot(h, wu_scr[...], preferred_element_type=F32)
    act = (a * jax.nn.sigmoid(a) * u).astype(BF16)
    ys_ref[...] = jnp.dot(act, wd_scr[...], preferred_element_type=F32)


def _experts(blk_e, xs, w_g, w_u, w_d, layer):
    _, _, D, FF = w_g.shape
    n_blk = blk_e.shape[0]
    return pl.pallas_call(
        _expert_kernel,
        grid_spec=pltpu.PrefetchScalarGridSpec(
            num_scalar_prefetch=1,
            grid=(n_blk,),
            in_specs=[pl.BlockSpec((MOE_BLOCK, D // 2), lambda i, be: (i, 0)),
                      pl.BlockSpec((1, 1, D, FF), lambda i, be: (layer, be[i], 0, 0)),
                      pl.BlockSpec((1, 1, D, FF), lambda i, be: (layer, be[i], 0, 0)),
                      pl.BlockSpec((1, 1, FF, D), lambda i, be: (layer, be[i], 0, 0))],
            out_specs=pl.BlockSpec((MOE_BLOCK, D), lambda i, be: (i, 0)),
            scratch_shapes=[pltpu.VMEM((D, FF), BF16), pltpu.VMEM((D, FF), BF16),
                            pltpu.VMEM((FF, D), BF16)]),
        out_shape=jax.ShapeDtypeStruct((n_blk * MOE_BLOCK, D), F32),
        compiler_params=_params(("arbitrary",)),
        name="experts",
    )(blk_e, xs, w_g, w_u, w_d)


def _combine_kernel(dest_ref, x_ref, w_ref, ys_hbm, o_ref, ybuf, sem, *, n_tok):
    tm = x_ref.shape[0]
    i = pl.program_id(0)

    def row_copy(step, r, k):
        d = dest_ref[k * n_tok + step * tm + r]
        buf = step % 2
        return pltpu.make_async_copy(ys_hbm.at[pl.ds(d, 1)], ybuf.at[buf, k, pl.ds(r, 1)],
                                     sem.at[buf])

    def start_step(step):
        def body(r, carry):
            row_copy(step, r, 0).start(priority=0)
            row_copy(step, r, 1).start(priority=1)
            return carry
        lax.fori_loop(0, tm, body, 0, unroll=8)

    @pl.when(i == 0)
    def _():
        start_step(0)

    @pl.when(i + 1 < pl.num_programs(0))
    def _():
        start_step(i + 1)

    def wait(r, carry):
        row_copy(i, r, 0).wait()
        row_copy(i, r, 1).wait()
        return carry

    lax.fori_loop(0, tm, wait, 0, unroll=8)
    w = w_ref[...]
    o_ref[...] = x_ref[...] + w[:, 0:1] * ybuf[i % 2, 0] + w[:, 1:2] * ybuf[i % 2, 1]


def _combine(dest_flat, x, w_cols, ys, tm):
    T, D = x.shape
    tm = min(tm, T)
    return pl.pallas_call(
        functools.partial(_combine_kernel, n_tok=T),
        grid_spec=pltpu.PrefetchScalarGridSpec(
            num_scalar_prefetch=1,
            grid=(T // tm,),
            in_specs=[pl.BlockSpec((tm, D), lambda i, d: (i, 0)),
                      pl.BlockSpec((tm, 2), lambda i, d: (i, 0)),
                      pl.BlockSpec(memory_space=pl.ANY)],
            out_specs=pl.BlockSpec((tm, D), lambda i, d: (i, 0)),
            scratch_shapes=[pltpu.VMEM((2, 2, tm, D), F32), pltpu.SemaphoreType.DMA((2,))]),
        out_shape=jax.ShapeDtypeStruct((T, D), F32),
        compiler_params=_params(("arbitrary",)),
        name="combine",
    )(dest_flat, x, w_cols, ys)


def _ple_kernel(x_ref, g_ref, p_ref, wg_ref, wp_ref, gn_ref, o_ref, *rest, tn, last):
    h_scr = rest[-1]
    rows, D = x_ref.shape
    _norm_rows_into(x_ref, g_ref, h_scr)
    pb = p_ref[...].astype(BF16)
    for j in range(D // tn):
        cols = slice(j * tn, (j + 1) * tn)
        gate = jax.nn.sigmoid(jnp.dot(h_scr[...], wg_ref[:, cols], preferred_element_type=F32))
        proj = jnp.dot(pb, wp_ref[:, cols], preferred_element_type=F32)
        o_ref[:, cols] = x_ref[:, cols] + gate * proj
    if last:
        step = min(NORM_ROWS, rows)

        def body(c, carry):
            r = pl.ds(pl.multiple_of(c * step, step), step)
            o_ref[r, :] = _rms(o_ref[r, :], gn_ref[...])
            return carry

        lax.fori_loop(0, rows // step, body, 0)
    else:
        _norm_rows_into(o_ref, gn_ref, rest[0])


def _ple(x, g, p, w_g, w_p, g_next, tm, tn, last):
    T, D = x.shape
    PD = p.shape[1]
    tm, tn = min(tm, T), min(tn, D)
    row_tile = pl.BlockSpec((tm, D), lambda i: (i, 0))
    gain = pl.BlockSpec((1, D), lambda i: (0, 0))
    return pl.pallas_call(
        functools.partial(_ple_kernel, tn=tn, last=last),
        grid=(T // tm,),
        in_specs=[row_tile, gain, pl.BlockSpec((tm, PD), lambda i: (i, 0)),
                  pl.BlockSpec((D, D), lambda i: (0, 0)), pl.BlockSpec((PD, D), lambda i: (0, 0)), gain],
        out_specs=[row_tile] if last else [row_tile, row_tile],
        out_shape=[jax.ShapeDtypeStruct((T, D), F32)] + ([] if last else [jax.ShapeDtypeStruct((T, D), BF16)]),
        scratch_shapes=[pltpu.VMEM((tm, D), BF16)],
        compiler_params=_params(("parallel",)),
        name="ple",
    )(x, g, p, w_g, w_p, g_next)


def _rope_tables(S):
    pos = np.arange(S)
    inv_freq = ROPE_THETA ** (-np.arange(ROPE_PAIRS, dtype=np.float32) / ROPE_PAIRS)
    ang_row = jnp.asarray((pos // GRID_W).astype(np.float32)[:, None] * inv_freq[None, :])
    ang_col = jnp.asarray((pos % GRID_W).astype(np.float32)[:, None] * inv_freq[None, :])
    zero = jnp.zeros_like(ang_row)
    cos_t = jnp.concatenate([jnp.cos(ang_row)] * 2 + [jnp.cos(ang_col)] * 2, axis=1)
    sin_lo = jnp.concatenate([-jnp.sin(ang_row), zero, -jnp.sin(ang_col), zero], axis=1)
    sin_hi = jnp.concatenate([zero, jnp.sin(ang_row), zero, jnp.sin(ang_col)], axis=1)
    return cos_t, sin_lo, sin_hi


def kernel(x, p, norm1_g, w_in, mlstm_gate_b, mlstm_norm_g, q_norm_g, k_norm_g, w_branch_m, w_branch_a, w_out, norm2_g, w_router_group, b_router_group, w_router_expert, b_router_expert, w_exp_gate, w_exp_up, w_exp_down, ple_norm_g, w_ple_gate, w_ple_proj, final_norm_g):
    B, S, D = x.shape
    depth = w_in.shape[0]
    T = B * S
    H = MLSTM_HEADS
    scale = HEAD_DIM ** -0.5
    tables = _rope_tables(S)
    row = lambda v: v.reshape(1, -1).astype(F32)

    o_qm, o_gates = 0, 4 * MLSTM_W
    o_qa = o_gates + 4 * H
    o_gm = o_qa + ATTN_Q_W + 2 * ATTN_KV_W
    o_ga = o_gm + D
    z_m = 2 * D
    z_qa = z_m + 4 * MLSTM_W

    xf = x.reshape(T, D)
    h = _norm_cast(xf, row(norm1_g[0]), 1024)
    for i in range(depth):
        wi = w_in[i]
        w_main = jnp.concatenate([wi[:, o_gm:], wi[:, o_qm:o_qm + MLSTM_W] * scale,
                                  wi[:, MLSTM_W:o_gates], wi[:, o_qa:o_gm]], axis=1).astype(BF16)
        lane_pad = ((0, 0), (0, GATE_LANES - 2 * H))
        w_gates = jnp.concatenate([jnp.pad(wi[:, o_gates:o_gates + 2 * H], lane_pad),
                                   jnp.pad(wi[:, o_gates + 2 * H:o_qa], lane_pad)], axis=1).astype(BF16)
        gb = mlstm_gate_b[i].astype(F32).reshape(1, 4 * H)
        gate_bias = jnp.concatenate([jnp.pad(gb[:, :2 * H], lane_pad),
                                     jnp.pad(gb[:, 2 * H:], lane_pad)], axis=1)
        z = _inproj(h, w_main, 4, 512)
        gates = _gates_proj(h, w_gates, 1024)
        z3 = z.reshape(B, S, -1)
        y_m = _mlstm(z3, z_m, gates.reshape(B, S, -1), gate_bias, row(mlstm_norm_g[i]))
        y_a = _attention(z3, z_qa // HEAD_DIM, tables, row(q_norm_g[i]) * (scale * np.log2(np.e)),
                         row(k_norm_g[i]), 256)
        xf = _mix(y_m.reshape(T, MLSTM_W), y_a.reshape(T, ATTN_Q_W), z, 0, 1,
                  w_branch_m[i].astype(BF16), w_branch_a[i].astype(BF16), w_out[i].astype(BF16),
                  xf, 256, 512)

        wr = jnp.zeros((ROUTER_ROWS, D), F32)
        wr = wr.at[:N_GROUPS].set(w_router_group[i].T)
        wr = wr.at[ROUTER_EXPERT_ROW0:ROUTER_EXPERT_ROW0 + N_EXPERTS].set(w_router_expert[i].T)
        br = jnp.zeros((ROUTER_ROWS, 1), F32)
        br = br.at[:N_GROUPS, 0].set(b_router_group[i])
        br = br.at[ROUTER_EXPERT_ROW0:ROUTER_EXPERT_ROW0 + N_EXPERTS, 0].set(b_router_expert[i])
        idx, wts, cnt, hp = _router(xf, row(norm2_g[i]), wr, br, 512)
        n_blk = -(-2 * T // MOE_BLOCK) + N_EXPERTS
        dest8, blk = _slots(idx, cnt, n_blk, 4096)
        dest = dest8[0:2].reshape(-1)
        xs = _dispatch(dest, hp, jnp.zeros((n_blk * MOE_BLOCK, D // 2), jnp.uint32), 512)
        ys = _experts(blk[0, :n_blk], xs, w_exp_gate, w_exp_up, w_exp_down, i)
        xf = _combine(dest, xf, wts[0:2].T, ys, 256)

        last = i == depth - 1
        out = _ple(xf, row(ple_norm_g[i]), p[i].reshape(T, -1), w_ple_gate[i].astype(BF16),
                   w_ple_proj[i].astype(BF16), row(final_norm_g if last else norm1_g[i + 1]),
                   512, 512, last)
        xf = out[0]
        if not last:
            h = out[1]

    return xf.reshape(B, S, D)
```

```python
import functools

import jax
import jax.numpy as jnp
import numpy as np
from jax import lax
from jax.experimental import pallas as pl
from jax.experimental.pallas import tpu as pltpu

F32 = jnp.float32
BF16 = jnp.bfloat16

GRID_W = 64
HEAD_DIM = 128
MLSTM_HEADS = 8
MLSTM_W = MLSTM_HEADS * HEAD_DIM
MLSTM_CHUNK = 128
MLSTM_HEADS_PER_STEP = 4
MLSTM_UNROLL = 4
MLSTM_DEN_ROWS = 16
ATTN_HEADS = 8
ATTN_KV_HEADS = 2
ATTN_GROUP = ATTN_HEADS // ATTN_KV_HEADS
ATTN_Q_W = ATTN_HEADS * HEAD_DIM
ATTN_KV_W = ATTN_KV_HEADS * HEAD_DIM
ROPE_THETA = 10000.0
ROPE_PAIRS = HEAD_DIM // 4
N_GROUPS = 4
EXPERTS_PER_GROUP = 8
N_EXPERTS = N_GROUPS * EXPERTS_PER_GROUP
MOE_BLOCK = 256
EPS = 1e-6
GATE_LANES = 128
ROUTER_ROWS = 128
ROUTER_EXPERT_ROW0 = 8
VMEM_LIMIT_BYTES = 52 * 1024 * 1024
NORM_ROWS = 128


def _params(semantics, **kw):
    return pltpu.CompilerParams(dimension_semantics=semantics,
                                vmem_limit_bytes=VMEM_LIMIT_BYTES, **kw)


def _rms(x, g):
    ms = jnp.mean(x * x, axis=-1, keepdims=True)
    return (x * lax.rsqrt(ms + EPS)) * g


def _norm_rows_into(x_ref, g_ref, h_scr):
    rows = x_ref.shape[0]
    step = min(NORM_ROWS, rows)

    def body(c, carry):
        r = pl.ds(pl.multiple_of(c * step, step), step)
        h_scr[r, :] = _rms(x_ref[r, :], g_ref[...]).astype(h_scr.dtype)
        return carry

    lax.fori_loop(0, rows // step, body, 0)


def _norm_cast_kernel(x_ref, g_ref, h_ref):
    h_ref[...] = _rms(x_ref[...], g_ref[...]).astype(h_ref.dtype)


def _norm_cast(x, g, tm):
    T, D = x.shape
    tm = min(tm, T)
    return pl.pallas_call(
        _norm_cast_kernel,
        grid=(T // tm,),
        in_specs=[pl.BlockSpec((tm, D), lambda i: (i, 0)), pl.BlockSpec((1, D), lambda i: (0, 0))],
        out_specs=pl.BlockSpec((tm, D), lambda i: (i, 0)),
        out_shape=jax.ShapeDtypeStruct((T, D), BF16),
        compiler_params=_params(("parallel",)),
        name="norm_cast",
    )(x, g)


def _matmul_kernel(h_ref, w_ref, o_ref):
    o_ref[...] = jnp.dot(h_ref[...], w_ref[...], preferred_element_type=F32).astype(o_ref.dtype)


def _inproj(h, w, col_groups, tm):
    T, D = h.shape
    N = w.shape[1]
    tm = min(tm, T)
    tn = N // col_groups
    assert tn * col_groups == N and tn % 128 == 0
    return pl.pallas_call(
        _matmul_kernel,
        grid=(col_groups, T // tm),
        in_specs=[pl.BlockSpec((tm, D), lambda j, i: (i, 0)),
                  pl.BlockSpec((D, tn), lambda j, i: (0, j))],
        out_specs=pl.BlockSpec((tm, tn), lambda j, i: (i, j)),
        out_shape=jax.ShapeDtypeStruct((T, N), BF16),
        compiler_params=_params(("arbitrary", "arbitrary")),
        name="inproj",
    )(h, w)


def _gates_proj(h, wg, tm):
    T, D = h.shape
    N = wg.shape[1]
    tm = min(tm, T)
    return pl.pallas_call(
        _matmul_kernel,
        grid=(T // tm,),
        in_specs=[pl.BlockSpec((tm, D), lambda i: (i, 0)), pl.BlockSpec((D, N), lambda i: (0, 0))],
        out_specs=pl.BlockSpec((tm, N), lambda i: (i, 0)),
        out_shape=jax.ShapeDtypeStruct((T, N), F32),
        compiler_params=_params(("parallel",)),
        name="gates_proj",
    )(h, wg)


def _log_sigmoid(x):
    return jnp.minimum(x, 0.0) - jnp.log1p(jnp.exp(-jnp.abs(x)))


def _mlstm_kernel(q_ref, k_ref, v_ref, o_ref, gi_ref, gf_ref, bi_ref, bf_ref, ng_ref, y_ref,
                  up_scr, bt_scr, ut_scr, cm_scr, qt_scr, vxt_scr, a_scr, ht_scr, c_scr):
    L, d = MLSTM_CHUNK, HEAD_DIM
    S = q_ref.shape[1]
    nc = S // L
    heads = q_ref.shape[2] // d
    head0 = pl.program_id(1) * heads
    srow = lax.broadcasted_iota(jnp.int32, (L, L), 0)
    tcol = lax.broadcasted_iota(jnp.int32, (L, L), 1)
    mask2 = jnp.concatenate([srow <= tcol, srow >= tcol], axis=1)
    lasts = (L - 1, 0)
    tri2 = jnp.concatenate([jnp.where(tcol <= srow, 1.0, 0.0),
                            jnp.where(tcol >= srow, 1.0, 0.0)], axis=0).astype(BF16)
    fw_lane = lax.broadcasted_iota(jnp.int32, (L, GATE_LANES), 1) < MLSTM_HEADS
    ones_t = jnp.ones((MLSTM_DEN_ROWS, L), BF16)
    ch_row = lax.broadcasted_iota(jnp.int32, (GATE_LANES, 2 * L), 0)
    ch_fw = lax.broadcasted_iota(jnp.int32, (GATE_LANES, 2 * L), 1) < L

    def split3(x):
        p1 = x.astype(BF16)
        r1 = x - p1.astype(F32)
        p2 = r1.astype(BF16)
        p3 = (r1 - p2.astype(F32)).astype(BF16)
        return p1, p2, p3

    def chunk(c):
        return pl.ds(pl.multiple_of(c * L, L), L)

    def gates_pass(c, carry):
        sl = chunk(c)
        ls = _log_sigmoid(gf_ref[0, sl, :] + bf_ref[...])
        cs = sum(jnp.dot(tri2, part, preferred_element_type=F32) for part in split3(ls))
        bn = jnp.where(fw_lane, cs[:L], cs[L:])
        un = gi_ref[0, sl, :] + bi_ref[...] - bn
        for i, part in enumerate(split3(un)):
            up_scr[i, sl, :] = part
        bt_scr[c] = bn.T[:2 * MLSTM_HEADS]
        ut_scr[c] = un.T[:2 * MLSTM_HEADS]
        return carry

    def head_passes(hh):
        hl = slice(hh * d, (hh + 1) * d)
        head = head0 + hh
        sel2 = jnp.where(ch_row == jnp.where(ch_fw, head, MLSTM_HEADS + head), 1.0, 0.0).astype(BF16)

        def intra(c, carry):
            sl = chunk(c)
            q_t = q_ref[0, sl, hl].astype(F32).T.astype(BF16)
            vx_t = jnp.concatenate([v_ref[0, sl, hl].astype(F32).T.astype(BF16), ones_t], axis=0)
            qt_scr[c] = q_t
            vxt_scr[c] = vx_t
            kq = jnp.dot(k_ref[0, sl, hl], q_t, preferred_element_type=F32)
            ucol = sum(jnp.dot(up_scr[i, sl, :], sel2, preferred_element_type=F32)
                       for i in range(3))
            cmu = jnp.max(jnp.where(mask2, ucol, -jnp.inf), axis=0, keepdims=True)
            e0 = jnp.where(mask2, jnp.exp(ucol - cmu), 0.0)
            s0 = (jnp.concatenate([kq, kq], axis=1) * e0).astype(BF16)
            a_scr[c] = jnp.dot(vx_t, s0, preferred_element_type=F32)
            cm_scr[c] = cmu
            return carry

        def recurrence(j, ms):
            out = []
            for dd in range(2):
                c = j if dd == 0 else nc - 1 - j
                tl = slice(dd * L, (dd + 1) * L)
                m = ms[dd]
                row = pl.ds(dd * MLSTM_HEADS + head, 1)
                urow = ut_scr[c, row, :]
                brow = bt_scr[c, row, :]
                cmu = cm_scr[c, :, tl]
                b_last = brow[:, lasts[dd]:lasts[dd] + 1]
                u_max = cmu[:, lasts[dd]:lasts[dd] + 1]
                mm = jnp.maximum(cmu, m)
                cx = c_scr[dd]
                numx = (a_scr[c, :, tl] * jnp.exp(cmu - mm) + jnp.exp(m - mm)
                        * jnp.dot(cx.astype(BF16), qt_scr[c], preferred_element_type=F32))
                den = jnp.maximum(jnp.abs(numx[d:d + 1]), jnp.exp(-(brow + mm)))
                ht_scr[dd, c] = numx[:d] * (1.0 / den)
                m_new = b_last + jnp.maximum(m, u_max)
                wk = jnp.exp(b_last + urow - m_new)
                vw = (vxt_scr[c].astype(F32) * wk).astype(BF16)
                c_scr[dd] = (jnp.exp(b_last + m - m_new) * cx
                             + jnp.dot(vw, k_ref[0, chunk(c), hl], preferred_element_type=F32))
                out.append(m_new)
            return tuple(out)

        def finish(c, carry):
            sl = chunk(c)
            hn = _rms((ht_scr[0, c] + ht_scr[1, c]).T, ng_ref[:, hl])
            y_ref[0, sl, hl] = (hn * jax.nn.sigmoid(o_ref[0, sl, hl].astype(F32))).astype(y_ref.dtype)
            return carry

        c_scr[...] = jnp.zeros_like(c_scr)
        lax.fori_loop(0, nc, intra, 0, unroll=2 * MLSTM_UNROLL)
        lax.fori_loop(0, nc, recurrence, (jnp.zeros((1, 1), F32),) * 2, unroll=MLSTM_UNROLL)
        lax.fori_loop(0, nc, finish, 0, unroll=MLSTM_UNROLL)

    lax.fori_loop(0, nc, gates_pass, 0, unroll=MLSTM_UNROLL)
    for hh in range(heads):
        head_passes(hh)


def _mlstm(z3, col0, gates3, gate_bias, norm_g):
    B, S, _ = z3.shape
    H, hps = MLSTM_HEADS, MLSTM_HEADS_PER_STEP
    L, d = MLSTM_CHUNK, HEAD_DIM
    nc = S // L
    w = hps * d
    rows = d + MLSTM_DEN_ROWS
    blk0 = col0 // w
    assert blk0 * w == col0
    col = lambda part: pl.BlockSpec((1, S, w), lambda b, h: (b, 0, blk0 + part * (H // hps) + h))
    gate = lambda part: pl.BlockSpec((1, S, GATE_LANES), lambda b, h: (b, 0, part))
    bias = lambda part: pl.BlockSpec((1, GATE_LANES), lambda b, h: (0, part))
    return pl.pallas_call(
        _mlstm_kernel,
        grid=(B, H // hps),
        in_specs=[col(0), col(1), col(2), col(3), gate(0), gate(1), bias(0), bias(1),
                  pl.BlockSpec((1, w), lambda b, h: (0, h))],
        out_specs=pl.BlockSpec((1, S, w), lambda b, h: (b, 0, h)),
        out_shape=jax.ShapeDtypeStruct((B, S, MLSTM_W), BF16),
        scratch_shapes=[pltpu.VMEM((3, S, GATE_LANES), BF16),
                        pltpu.VMEM((nc, 2 * H, L), F32),
                        pltpu.VMEM((nc, 2 * H, L), F32),
                        pltpu.VMEM((nc, 1, 2 * L), F32),
                        pltpu.VMEM((nc, d, L), BF16),
                        pltpu.VMEM((nc, rows, L), BF16),
                        pltpu.VMEM((nc, rows, 2 * L), F32),
                        pltpu.VMEM((2, nc, d, L), F32),
                        pltpu.VMEM((2, rows, d), F32)],
        compiler_params=_params(("parallel", "arbitrary")),
        name="mlstm",
    )(z3, z3, z3, z3, gates3, gates3, gate_bias, gate_bias, norm_g)


def _rope(t, c, s_lo, s_hi):
    quarter = HEAD_DIM // 4
    return (t * c + pltpu.roll(t, HEAD_DIM - quarter, 1) * s_lo
            + pltpu.roll(t, quarter, 1) * s_hi)


def _attn_kernel(q_ref, k_ref, v_ref, cq_ref, sloq_ref, shiq_ref, ck_ref, slok_ref, shik_ref,
                 qg_ref, kg_ref, o_ref, k_scr):
    S = k_ref.shape[1]
    step = min(256, S)

    @pl.when(pl.program_id(2) == 0)
    def _():
        def body(c, carry):
            r = pl.ds(pl.multiple_of(c * step, step), step)
            kn = _rms(k_ref[0, r, :].astype(F32), kg_ref[...])
            k_scr[r, :] = _rope(kn, ck_ref[r, :], slok_ref[r, :], shik_ref[r, :]).astype(BF16)
            return carry

        lax.fori_loop(0, S // step, body, 0)

    for g in range(ATTN_GROUP):
        hl = slice(g * HEAD_DIM, (g + 1) * HEAD_DIM)
        qn = _rms(q_ref[0, :, hl].astype(F32), qg_ref[...])
        qr = _rope(qn, cq_ref[...], sloq_ref[...], shiq_ref[...]).astype(BF16)
        s = lax.dot_general(qr, k_scr[...], (((1,), (1,)), ((), ())), preferred_element_type=F32)
        p = jnp.exp2(s - jnp.max(s, axis=1, keepdims=True))
        l = jnp.sum(p, axis=1, keepdims=True)
        o = jnp.dot(p.astype(BF16), v_ref[0], preferred_element_type=F32)
        o_ref[0, :, hl] = (o / l).astype(o_ref.dtype)


def _attention(z3, q_blk0, tables, q_g, k_g, tq):
    B, S, _ = z3.shape
    tq = min(tq, S)
    cos_t, slo_t, shi_t = tables
    gw = ATTN_GROUP * HEAD_DIM
    q_grp0 = q_blk0 // ATTN_GROUP
    assert q_grp0 * ATTN_GROUP == q_blk0
    k_blk0 = q_blk0 + ATTN_HEADS
    v_blk0 = k_blk0 + ATTN_KV_HEADS
    qtab = pl.BlockSpec((tq, HEAD_DIM), lambda b, kv, i: (i, 0))
    ktab = pl.BlockSpec((S, HEAD_DIM), lambda b, kv, i: (0, 0))
    gain = pl.BlockSpec((1, HEAD_DIM), lambda b, kv, i: (0, 0))
    return pl.pallas_call(
        _attn_kernel,
        grid=(B, ATTN_KV_HEADS, S // tq),
        in_specs=[pl.BlockSpec((1, tq, gw), lambda b, kv, i: (b, i, q_grp0 + kv)),
                  pl.BlockSpec((1, S, HEAD_DIM), lambda b, kv, i: (b, 0, k_blk0 + kv)),
                  pl.BlockSpec((1, S, HEAD_DIM), lambda b, kv, i: (b, 0, v_blk0 + kv)),
                  qtab, qtab, qtab, ktab, ktab, ktab, gain, gain],
        out_specs=pl.BlockSpec((1, tq, gw), lambda b, kv, i: (b, i, kv)),
        out_shape=jax.ShapeDtypeStruct((B, S, ATTN_Q_W), BF16),
        scratch_shapes=[pltpu.VMEM((S, HEAD_DIM), BF16)],
        compiler_params=_params(("parallel", "parallel", "arbitrary")),
        name="attention",
    )(z3, z3, z3, cos_t, slo_t, shi_t, cos_t, slo_t, shi_t, q_g, k_g)


def _mix_kernel(ym_ref, ya_ref, gm_ref, ga_ref, wm_ref, wa_ref, wo_ref, x_ref, o_ref, mg_scr, *, tn):
    D = x_ref.shape[1]
    for j in range(D // tn):
        cols = slice(j * tn, (j + 1) * tn)
        bm = jnp.dot(ym_ref[...], wm_ref[:, cols], preferred_element_type=F32)
        ba = jnp.dot(ya_ref[...], wa_ref[:, cols], preferred_element_type=F32)
        mg_scr[:, cols] = (jax.nn.sigmoid(gm_ref[:, cols].astype(F32)) * bm
                           + jax.nn.sigmoid(ga_ref[:, cols].astype(F32)) * ba).astype(mg_scr.dtype)
    for j in range(D // tn):
        cols = slice(j * tn, (j + 1) * tn)
        o_ref[:, cols] = x_ref[:, cols] + jnp.dot(mg_scr[...], wo_ref[:, cols],
                                                  preferred_element_type=F32)


def _mix(y_m, y_a, z, gm_blk, ga_blk, w_m, w_a, w_o, x, tm, tn):
    T, K = y_m.shape
    D = x.shape[1]
    tm, tn = min(tm, T), min(tn, D)
    return pl.pallas_call(
        functools.partial(_mix_kernel, tn=tn),
        grid=(T // tm,),
        in_specs=[pl.BlockSpec((tm, K), lambda i: (i, 0)),
                  pl.BlockSpec((tm, K), lambda i: (i, 0)),
                  pl.BlockSpec((tm, D), lambda i: (i, gm_blk)),
                  pl.BlockSpec((tm, D), lambda i: (i, ga_blk)),
                  pl.BlockSpec((K, D), lambda i: (0, 0)),
                  pl.BlockSpec((K, D), lambda i: (0, 0)),
                  pl.BlockSpec((D, D), lambda i: (0, 0)),
                  pl.BlockSpec((tm, D), lambda i: (i, 0))],
        out_specs=pl.BlockSpec((tm, D), lambda i: (i, 0)),
        out_shape=jax.ShapeDtypeStruct((T, D), F32),
        scratch_shapes=[pltpu.VMEM((tm, D), BF16)],
        compiler_params=_params(("parallel",)),
        name="mix",
    )(y_m, y_a, z, z, w_m, w_a, w_o, x)


def _first_argmax(vals, vmax):
    idx = jnp.full(vals[0].shape, len(vals), jnp.int32)
    for j in reversed(range(len(vals))):
        idx = jnp.where(vals[j] == vmax, j, idx)
    return idx


def _pack_bf16_pairs(h):
    n = h.shape[1] // 2
    hb = h.astype(BF16).astype(F32)
    lo = lax.bitcast_convert_type(hb[:, :n], jnp.uint32) >> 16
    hi = lax.bitcast_convert_type(hb[:, n:], jnp.uint32) & jnp.uint32(0xFFFF0000)
    return lo | hi


def _unpack_bf16_pairs(p):
    lo = lax.bitcast_convert_type(p << 16, F32)
    hi = lax.bitcast_convert_type(p & jnp.uint32(0xFFFF0000), F32)
    return jnp.concatenate([lo, hi], axis=1).astype(BF16)


def _router_kernel(x_ref, g_ref, wr_ref, br_ref, idx_ref, wts_ref, cnt_ref, hp_ref,
                   tri_scr, carry_scr):
    tm = x_ref.shape[0]

    @pl.when(pl.program_id(0) == 0)
    def _():
        r = lax.broadcasted_iota(jnp.int32, (tm, tm), 0)
        c = lax.broadcasted_iota(jnp.int32, (tm, tm), 1)
        tri_scr[...] = jnp.where(r < c, 1.0, 0.0).astype(BF16)
        carry_scr[...] = jnp.zeros_like(carry_scr)

    h = _rms(x_ref[...], g_ref[...])
    hp_ref[...] = _pack_bf16_pairs(h)
    logits = lax.dot_general(wr_ref[...], h, (((1,), (1,)), ((), ())),
                             preferred_element_type=F32,
                             precision=lax.Precision.HIGHEST) + br_ref[...]
    gl = [logits[j:j + 1, :] for j in range(N_GROUPS)]
    gmax = functools.reduce(jnp.maximum, gl)
    gi = _first_argmax(gl, gmax)
    gp = 1.0 / functools.reduce(jnp.add, [jnp.exp(v - gmax) for v in gl])
    eg = []
    for e in range(EXPERTS_PER_GROUP):
        v = logits[ROUTER_EXPERT_ROW0 + e:ROUTER_EXPERT_ROW0 + e + 1, :]
        for grp in range(1, N_GROUPS):
            r0 = ROUTER_EXPERT_ROW0 + grp * EXPERTS_PER_GROUP + e
            v = jnp.where(gi == grp, logits[r0:r0 + 1, :], v)
        eg.append(v)
    v1 = functools.reduce(jnp.maximum, eg)
    i1 = _first_argmax(eg, v1)
    eg2 = [jnp.where(i1 == e, -jnp.inf, eg[e]) for e in range(EXPERTS_PER_GROUP)]
    v2 = functools.reduce(jnp.maximum, eg2)
    i2 = _first_argmax(eg2, v2)
    e2 = jnp.exp(v2 - v1)
    w1 = gp / (1.0 + e2)
    w2 = gp * e2 / (1.0 + e2)
    eid1 = gi * EXPERTS_PER_GROUP + i1
    eid2 = gi * EXPERTS_PER_GROUP + i2

    erow = lax.broadcasted_iota(jnp.int32, (N_EXPERTS, tm), 0)
    oh1 = erow == eid1
    oh2 = erow == eid2
    oh = jnp.where(oh1 | oh2, 1.0, 0.0)
    before = jnp.dot(oh.astype(BF16), tri_scr[...], preferred_element_type=F32) + carry_scr[...]
    rank1 = jnp.sum(jnp.where(oh1, before, 0.0), axis=0, keepdims=True)
    rank2 = jnp.sum(jnp.where(oh2, before, 0.0), axis=0, keepdims=True)
    carry_scr[...] = carry_scr[...] + jnp.sum(oh, axis=1, keepdims=True)

    idx_ref[...] = jnp.zeros_like(idx_ref)
    idx_ref[0:1, :] = eid1
    idx_ref[1:2, :] = eid2
    idx_ref[2:3, :] = rank1.astype(jnp.int32)
    idx_ref[3:4, :] = rank2.astype(jnp.int32)
    wts_ref[...] = jnp.zeros_like(wts_ref)
    wts_ref[0:1, :] = w1
    wts_ref[1:2, :] = w2
    cnt_ref[...] = jnp.broadcast_to(carry_scr[...], cnt_ref.shape).astype(jnp.int32)


def _router(x, g, wr, br, tm):
    T, D = x.shape
    tm = min(tm, T)
    return pl.pallas_call(
        _router_kernel,
        grid=(T // tm,),
        in_specs=[pl.BlockSpec((tm, D), lambda i: (i, 0)),
                  pl.BlockSpec((1, D), lambda i: (0, 0)),
                  pl.BlockSpec((ROUTER_ROWS, D), lambda i: (0, 0)),
                  pl.BlockSpec((ROUTER_ROWS, 1), lambda i: (0, 0))],
        out_specs=[pl.BlockSpec((8, tm), lambda i: (0, i)),
                   pl.BlockSpec((8, tm), lambda i: (0, i)),
                   pl.BlockSpec((N_EXPERTS, 128), lambda i: (0, 0)),
                   pl.BlockSpec((tm, D // 2), lambda i: (i, 0))],
        out_shape=[jax.ShapeDtypeStruct((8, T), jnp.int32),
                   jax.ShapeDtypeStruct((8, T), F32),
                   jax.ShapeDtypeStruct((N_EXPERTS, 128), jnp.int32),
                   jax.ShapeDtypeStruct((T, D // 2), jnp.uint32)],
        scratch_shapes=[pltpu.VMEM((tm, tm), BF16), pltpu.VMEM((N_EXPERTS, 1), F32)],
        compiler_params=_params(("arbitrary",)),
        name="router",
    )(x, g, wr, br)


def _slots_kernel(idx_ref, cnt_ref, dest_ref, blk_ref):
    shift = MOE_BLOCK.bit_length() - 1
    cnt = cnt_ref[...]
    padded = ((cnt + (MOE_BLOCK - 1)) >> shift) << shift
    eid1, eid2 = idx_ref[0:1, :], idx_ref[1:2, :]
    d1, d2 = idx_ref[2:3, :], idx_ref[3:4, :]
    blk_start = lax.broadcasted_iota(jnp.int32, (1, blk_ref.shape[1]), 1) * MOE_BLOCK
    blk_e = jnp.zeros_like(blk_start)
    acc = jnp.zeros((1, 1), jnp.int32)
    for e in range(N_EXPERTS):
        d1 = d1 + jnp.where(eid1 == e, acc, 0)
        d2 = d2 + jnp.where(eid2 == e, acc, 0)
        acc = acc + padded[e:e + 1, 0:1]
        blk_e = blk_e + jnp.where(acc <= blk_start, 1, 0)
    dest_ref[...] = jnp.zeros_like(dest_ref)
    dest_ref[0:1, :] = d1
    dest_ref[1:2, :] = d2
    blk_ref[...] = jnp.zeros_like(blk_ref)
    blk_ref[0:1, :] = jnp.minimum(blk_e, N_EXPERTS - 1)


def _slots(idx, cnt, n_blk, tm):
    T = idx.shape[1]
    tm = min(tm, T)
    lanes = -(-n_blk // 128) * 128
    return pl.pallas_call(
        _slots_kernel,
        grid=(T // tm,),
        in_specs=[pl.BlockSpec((8, tm), lambda i: (0, i)),
                  pl.BlockSpec((N_EXPERTS, 128), lambda i: (0, 0))],
        out_specs=[pl.BlockSpec((8, tm), lambda i: (0, i)),
                   pl.BlockSpec((8, lanes), lambda i: (0, 0))],
        out_shape=[jax.ShapeDtypeStruct((8, T), jnp.int32),
                   jax.ShapeDtypeStruct((8, lanes), jnp.int32)],
        compiler_params=_params(("arbitrary",)),
        name="slots",
    )(idx, cnt)


def _dispatch_kernel(dest_ref, hp_ref, xs_in, xs_hbm, sem, *, n_tok):
    del xs_in
    tm = hp_ref.shape[0]
    base = pl.program_id(0) * tm

    def row_copy(r, k):
        d = dest_ref[k * n_tok + base + r]
        return pltpu.make_async_copy(hp_ref.at[pl.ds(r, 1)], xs_hbm.at[pl.ds(d, 1)], sem)

    def start(r, carry):
        row_copy(r, 0).start(priority=0)
        row_copy(r, 1).start(priority=1)
        return carry

    def wait(r, carry):
        row_copy(r, 0).wait()
        row_copy(r, 1).wait()
        return carry

    lax.fori_loop(0, tm, start, 0, unroll=8)
    lax.fori_loop(0, tm, wait, 0, unroll=8)


def _dispatch(dest_flat, hp, xs_zero, tm):
    T, W = hp.shape
    tm = min(tm, T)
    return pl.pallas_call(
        functools.partial(_dispatch_kernel, n_tok=T),
        grid_spec=pltpu.PrefetchScalarGridSpec(
            num_scalar_prefetch=1,
            grid=(T // tm,),
            in_specs=[pl.BlockSpec((tm, W), lambda i, d: (i, 0)),
                      pl.BlockSpec(memory_space=pl.ANY)],
            out_specs=pl.BlockSpec(memory_space=pl.ANY),
            scratch_shapes=[pltpu.SemaphoreType.DMA]),
        out_shape=jax.ShapeDtypeStruct(xs_zero.shape, xs_zero.dtype),
        input_output_aliases={2: 0},
        compiler_params=_params(("arbitrary",)),
        name="dispatch",
    )(dest_flat, hp, xs_zero)


def _expert_kernel(be_ref, xs_ref, wg_ref, wu_ref, wd_ref, ys_ref, wg_scr, wu_scr, wd_scr):
    i = pl.program_id(0)

    @pl.when((i == 0) | (be_ref[i] != be_ref[jnp.maximum(i - 1, 0)]))
    def _():
        def cast(src, dst):
            step = 256
            def body(c, carry):
                r = pl.ds(pl.multiple_of(c * step, step), step)
                dst[r, :] = src[0, 0, r, :].astype(BF16)
                return carry
            lax.fori_loop(0, dst.shape[0] // step, body, 0)
        cast(wg_ref, wg_scr)
        cast(wu_ref, wu_scr)
        cast(wd_ref, wd_scr)

    h = _unpack_bf16_pairs(xs_ref[...])
    a = jnp.dot(h, wg_scr[...], preferred_element_type=F32)
    u = jnp.dot(h, wu_scr[...], preferred_element_type=F32)
    act = (a * jax.nn.sigmoid(a) * u).astype(BF16)
    ys_ref[...] = jnp.dot(act, wd_scr[...], preferred_element_type=F32)


def _experts(blk_e, xs, w_g, w_u, w_d, layer):
    _, _, D, FF = w_g.shape
    n_blk = blk_e.shape[0]
    return pl.pallas_call(
        _expert_kernel,
        grid_spec=pltpu.PrefetchScalarGridSpec(
            num_scalar_prefetch=1,
            grid=(n_blk,),
            in_specs=[pl.BlockSpec((MOE_BLOCK, D // 2), lambda i, be: (i, 0)),
                      pl.BlockSpec((1, 1, D, FF), lambda i, be: (layer, be[i], 0, 0)),
                      pl.BlockSpec((1, 1, D, FF), lambda i, be: (layer, be[i], 0, 0)),
                      pl.BlockSpec((1, 1, FF, D), lambda i, be: (layer, be[i], 0, 0))],
            out_specs=pl.BlockSpec((MOE_BLOCK, D), lambda i, be: (i, 0)),
            scratch_shapes=[pltpu.VMEM((D, FF), BF16), pltpu.VMEM((D, FF), BF16),
                            pltpu.VMEM((FF, D), BF16)]),
        out_shape=jax.ShapeDtypeStruct((n_blk * MOE_BLOCK, D), F32),
        compiler_params=_params(("arbitrary",)),
        name="experts",
    )(blk_e, xs, w_g, w_u, w_d)


def _combine_ple_kernel(dest_ref, x_ref, w_ref, ys_hbm, g_ref, p_ref, wg_ref, wp_ref, gn_ref,
                        o_ref, *rest, tn, last, n_tok):
    ybuf, xn_scr, h_scr, sem = rest[-4:]
    tm, D = x_ref.shape
    i = pl.program_id(0)

    def row_copy(step, r, k):
        d = dest_ref[k * n_tok + step * tm + r]
        buf = step % 2
        return pltpu.make_async_copy(ys_hbm.at[pl.ds(d, 1)], ybuf.at[buf, k, pl.ds(r, 1)],
                                     sem.at[buf])

    def start_step(step):
        def body(r, carry):
            row_copy(step, r, 0).start(priority=0)
            row_copy(step, r, 1).start(priority=1)
            return carry
        lax.fori_loop(0, tm, body, 0, unroll=8)

    @pl.when(i == 0)
    def _():
        start_step(0)

    @pl.when(i + 1 < pl.num_programs(0))
    def _():
        start_step(i + 1)

    def wait(r, carry):
        row_copy(i, r, 0).wait()
        row_copy(i, r, 1).wait()
        return carry

    lax.fori_loop(0, tm, wait, 0, unroll=8)

    step = min(NORM_ROWS, tm)

    def row_chunks(fn):
        def body(c, carry):
            fn(pl.ds(pl.multiple_of(c * step, step), step))
            return carry
        lax.fori_loop(0, tm // step, body, 0)

    def combine(r):
        w = w_ref[r, :]
        xn_scr[r, :] = x_ref[r, :] + w[:, 0:1] * ybuf[i % 2, 0, r, :] + w[:, 1:2] * ybuf[i % 2, 1, r, :]

    row_chunks(combine)
    _norm_rows_into(xn_scr, g_ref, h_scr)
    pb = p_ref[...].astype(BF16)
    for j in range(D // tn):
        cols = slice(j * tn, (j + 1) * tn)
        gate = jax.nn.sigmoid(jnp.dot(h_scr[...], wg_ref[:, cols], preferred_element_type=F32))
        proj = jnp.dot(pb, wp_ref[:, cols], preferred_element_type=F32)
        o_ref[:, cols] = xn_scr[:, cols] + gate * proj
    if last:
        def final(r):
            o_ref[r, :] = _rms(o_ref[r, :], gn_ref[...])
        row_chunks(final)
    else:
        _norm_rows_into(o_ref, gn_ref, rest[0])


def _combine_ple(dest_flat, x, w_cols, ys, g, p, w_g, w_p, g_next, tm, tn, last):
    T, D = x.shape
    PD = p.shape[1]
    tm, tn = min(tm, T), min(tn, D)
    row_tile = pl.BlockSpec((tm, D), lambda i, d: (i, 0))
    gain = pl.BlockSpec((1, D), lambda i, d: (0, 0))
    return pl.pallas_call(
        functools.partial(_combine_ple_kernel, tn=tn, last=last, n_tok=T),
        grid_spec=pltpu.PrefetchScalarGridSpec(
            num_scalar_prefetch=1,
            grid=(T // tm,),
            in_specs=[row_tile, pl.BlockSpec((tm, 2), lambda i, d: (i, 0)),
                      pl.BlockSpec(memory_space=pl.ANY), gain,
                      pl.BlockSpec((tm, PD), lambda i, d: (i, 0)),
                      pl.BlockSpec((D, D), lambda i, d: (0, 0)),
                      pl.BlockSpec((PD, D), lambda i, d: (0, 0)), gain],
            out_specs=[row_tile] if last else [row_tile, row_tile],
            scratch_shapes=[pltpu.VMEM((2, 2, tm, D), F32), pltpu.VMEM((tm, D), F32),
                            pltpu.VMEM((tm, D), BF16), pltpu.SemaphoreType.DMA((2,))]),
        out_shape=[jax.ShapeDtypeStruct((T, D), F32)] + ([] if last else [jax.ShapeDtypeStruct((T, D), BF16)]),
        compiler_params=_params(("arbitrary",)),
        name="combine_ple",
    )(dest_flat, x, w_cols, ys, g, p, w_g, w_p, g_next)


def _rope_tables(S):
    pos = np.arange(S)
    inv_freq = ROPE_THETA ** (-np.arange(ROPE_PAIRS, dtype=np.float32) / ROPE_PAIRS)
    ang_row = jnp.asarray((pos // GRID_W).astype(np.float32)[:, None] * inv_freq[None, :])
    ang_col = jnp.asarray((pos % GRID_W).astype(np.float32)[:, None] * inv_freq[None, :])
    zero = jnp.zeros_like(ang_row)
    cos_t = jnp.concatenate([jnp.cos(ang_row)] * 2 + [jnp.cos(ang_col)] * 2, axis=1)
    sin_lo = jnp.concatenate([-jnp.sin(ang_row), zero, -jnp.sin(ang_col), zero], axis=1)
    sin_hi = jnp.concatenate([zero, jnp.sin(ang_row), zero, jnp.sin(ang_col)], axis=1)
    return cos_t, sin_lo, sin_hi


def kernel(x, p, norm1_g, w_in, mlstm_gate_b, mlstm_norm_g, q_norm_g, k_norm_g, w_branch_m, w_branch_a, w_out, norm2_g, w_router_group, b_router_group, w_router_expert, b_router_expert, w_exp_gate, w_exp_up, w_exp_down, ple_norm_g, w_ple_gate, w_ple_proj, final_norm_g):
    B, S, D = x.shape
    depth = w_in.shape[0]
    T = B * S
    H = MLSTM_HEADS
    scale = HEAD_DIM ** -0.5
    tables = _rope_tables(S)
    row = lambda v: v.reshape(1, -1).astype(F32)

    o_qm, o_gates = 0, 4 * MLSTM_W
    o_qa = o_gates + 4 * H
    o_gm = o_qa + ATTN_Q_W + 2 * ATTN_KV_W
    o_ga = o_gm + D
    z_m = 2 * D
    z_qa = z_m + 4 * MLSTM_W

    xf = x.reshape(T, D)
    h = _norm_cast(xf, row(norm1_g[0]), 1024)
    for i in range(depth):
        wi = w_in[i]
        w_main = jnp.concatenate([wi[:, o_gm:], wi[:, o_qm:o_qm + MLSTM_W] * scale,
                                  wi[:, MLSTM_W:o_gates], wi[:, o_qa:o_gm]], axis=1).astype(BF16)
        lane_pad = ((0, 0), (0, GATE_LANES - 2 * H))
        w_gates = jnp.concatenate([jnp.pad(wi[:, o_gates:o_gates + 2 * H], lane_pad),
                                   jnp.pad(wi[:, o_gates + 2 * H:o_qa], lane_pad)], axis=1).astype(BF16)
        gb = mlstm_gate_b[i].astype(F32).reshape(1, 4 * H)
        gate_bias = jnp.concatenate([jnp.pad(gb[:, :2 * H], lane_pad),
                                     jnp.pad(gb[:, 2 * H:], lane_pad)], axis=1)
        z = _inproj(h, w_main, 4, 512)
        gates = _gates_proj(h, w_gates, 1024)
        z3 = z.reshape(B, S, -1)
        y_m = _mlstm(z3, z_m, gates.reshape(B, S, -1), gate_bias, row(mlstm_norm_g[i]))
        y_a = _attention(z3, z_qa // HEAD_DIM, tables, row(q_norm_g[i]) * (scale * np.log2(np.e)),
                         row(k_norm_g[i]), 256)
        xf = _mix(y_m.reshape(T, MLSTM_W), y_a.reshape(T, ATTN_Q_W), z, 0, 1,
                  w_branch_m[i].astype(BF16), w_branch_a[i].astype(BF16), w_out[i].astype(BF16),
                  xf, 256, 512)

        wr = jnp.zeros((ROUTER_ROWS, D), F32)
        wr = wr.at[:N_GROUPS].set(w_router_group[i].T)
        wr = wr.at[ROUTER_EXPERT_ROW0:ROUTER_EXPERT_ROW0 + N_EXPERTS].set(w_router_expert[i].T)
        br = jnp.zeros((ROUTER_ROWS, 1), F32)
        br = br.at[:N_GROUPS, 0].set(b_router_group[i])
        br = br.at[ROUTER_EXPERT_ROW0:ROUTER_EXPERT_ROW0 + N_EXPERTS, 0].set(b_router_expert[i])
        idx, wts, cnt, hp = _router(xf, row(norm2_g[i]), wr, br, 512)
        n_blk = -(-2 * T // MOE_BLOCK) + N_EXPERTS
        dest8, blk = _slots(idx, cnt, n_blk, 4096)
        dest = dest8[0:2].reshape(-1)
        xs = _dispatch(dest, hp, jnp.zeros((n_blk * MOE_BLOCK, D // 2), jnp.uint32), 512)
        ys = _experts(blk[0, :n_blk], xs, w_exp_gate, w_exp_up, w_exp_down, i)

        last = i == depth - 1
        out = _combine_ple(dest, xf, wts[0:2].T, ys, row(ple_norm_g[i]), p[i].reshape(T, -1),
                           w_ple_gate[i].astype(BF16), w_ple_proj[i].astype(BF16),
                           row(final_norm_g if last else norm1_g[i + 1]), 256, 512, last)
        xf = out[0]
        if not last:
            h = out[1]

    return xf.reshape(B, S, D)
```
